```python
import jax, jax.numpy as jnp
from jax import lax
import numpy as np

D_MODEL = 2048
BATCH = 4
SEQ = 2048
DEPTH = 4
DEC_BATCH = 8
DEC_SEQ = 1
PAST_LEN = 16384
PAGE_SIZE = 128

HEAD_DIM = 128
D_BR = D_MODEL // 2
N_HEADS = D_BR // HEAD_DIM
D_CONV = D_BR
N_BRANCH = 3
CONV_WIDTH = 31
MOBA_BLOCK = 256
MOBA_TOPK = 3
SB_Q_BLOCK = 128
MOBA_Q_BLOCK = 16
D_FF = ((8 * D_MODEL // 3 + 127) // 128) * 128
D_PLE = 256
D_IN = 8 * D_BR
RMS_EPS = 1e-6
LN_EPS = 1e-5

kernel_name = 'hybrid_conv_stickbreak_moba_decoder_step'


def rms_norm(x, g):
    x32 = x.astype(jnp.float32)
    y = x32 * lax.rsqrt(jnp.mean(x32 * x32, axis=-1, keepdims=True) + RMS_EPS)
    return (y * g.astype(jnp.float32)).astype(x.dtype)


def layer_norm(x, g, b):
    x32 = x.astype(jnp.float32)
    mu = jnp.mean(x32, axis=-1, keepdims=True)
    var = jnp.mean(jnp.square(x32 - mu), axis=-1, keepdims=True)
    y = (x32 - mu) * lax.rsqrt(var + LN_EPS)
    return (y * g.astype(jnp.float32) + b.astype(jnp.float32)).astype(x.dtype)


def swiglu(x, w_up, w_down):
    a, b = jnp.split(x @ w_up, 2, axis=-1)
    return (jax.nn.silu(a) * b) @ w_down


def depthwise_causal_conv(buf, w, b):
    c = buf.shape[-1]
    out = lax.conv_general_dilated(buf, w[:, None, :], window_strides=(1,), padding='VALID',
                                   dimension_numbers=('NWC', 'WIO', 'NWC'), feature_group_count=c)
    return out + b


def gather_pages(cache_l, page_table):
    pages = cache_l[page_table]
    nb, npg, ps = pages.shape[:3]
    return pages.reshape(nb, npg * ps, *pages.shape[3:])


def sweep_queries(fn, q, q_pos, blk):
    n, tq = q.shape[:2]
    if tq <= blk or tq % blk != 0:
        return fn(q, q_pos)
    nq = tq // blk
    qb = q.reshape(n, nq, blk, *q.shape[2:]).swapaxes(0, 1)
    pb = q_pos.reshape(nq, blk)
    out = lax.map(lambda a: fn(a[0], a[1]), (qb, pb))
    return out.swapaxes(0, 1).reshape(n, tq, *out.shape[3:])


def stick_breaking(q, k, v, q_pos, k_pos):
    z = jnp.einsum('nqhd,nkhd->nhqk', q, k).astype(jnp.float32) * (HEAD_DIM ** -0.5)
    mask = (k_pos[None, :] < q_pos[:, None])[None, None]
    log_stay = jnp.where(mask, jax.nn.log_sigmoid(-z), 0.0)
    later = lax.cumsum(log_stay, axis=3, reverse=True) - log_stay
    a = jnp.where(mask, jnp.exp(jax.nn.log_sigmoid(z) + later), 0.0)
    out = jnp.einsum('nhqk,nkhd->nqhd', a, v.astype(jnp.float32))
    return out.astype(q.dtype)


def moba_prepare(k, v):
    n, tk, h, d = k.shape
    nb = -(-tk // MOBA_BLOCK)
    padw = ((0, 0), (0, nb * MOBA_BLOCK - tk), (0, 0), (0, 0))
    kb = jnp.pad(k, padw).reshape(n, nb, MOBA_BLOCK, h, d)
    vb = jnp.pad(v, padw).reshape(n, nb, MOBA_BLOCK, h, d)
    kmean = jnp.mean(kb.astype(jnp.float32), axis=2)
    return kb.transpose(0, 3, 1, 2, 4), vb.transpose(0, 3, 1, 2, 4), kmean


def moba_query(q, q_pos, kbh, vbh, kmean):
    n, tq, h, d = q.shape
    nb = kbh.shape[2]
    ksel = min(MOBA_TOPK, nb)
    q_blk = q_pos // MOBA_BLOCK
    gate = jnp.einsum('nqhd,nbhd->nqhb', q.astype(jnp.float32), kmean)
    fully_past = jnp.arange(nb)[None, :] < q_blk[:, None]
    gate = jnp.where(fully_past[None, :, None, :], gate, -jnp.inf)
    gval, gidx = lax.top_k(gate, ksel)
    own = jnp.broadcast_to(q_blk[None, :, None, None], (n, tq, h, 1)).astype(gidx.dtype)
    idx = jnp.concatenate([gidx, own], axis=-1)
    n_i = jnp.arange(n)[:, None, None, None]
    h_i = jnp.arange(h)[None, None, :, None]
    kg = kbh[n_i, h_i, idx]
    vg = vbh[n_i, h_i, idx]
    s = jnp.einsum('nqhd,nqhjsd->nqhjs', q, kg).astype(jnp.float32) * (HEAD_DIM ** -0.5)
    sel_ok = jnp.concatenate([jnp.isfinite(gval), jnp.ones((n, tq, h, 1), dtype=bool)], axis=-1)
    k_pos = idx[..., None] * MOBA_BLOCK + jnp.arange(MOBA_BLOCK)
    mask = sel_ok[..., None] & (k_pos <= q_pos[None, :, None, None, None])
    s = jnp.where(mask, s, -jnp.inf)
    pr = jax.nn.softmax(s.reshape(n, tq, h, -1), axis=-1).reshape(s.shape)
    return jnp.einsum('nqhjs,nqhjsd->nqhd', pr.astype(vg.dtype), vg)


def trunk_layer(x, p, conv_buf, past, w):
    n, t, _ = x.shape
    h = x + 0.5 * swiglu(rms_norm(x, w['n_ffn1']), w['ff1_in'], w['ff1_out'])
    u = rms_norm(h, w['n_mix'])
    a_val, a_gate, q_sb, k_sb, v_sb, q_mb, k_mb, v_mb = jnp.split(u @ w['w_in'], 8, axis=-1)
    q_sb, k_sb, v_sb, q_mb, k_mb, v_mb = [a.reshape(n, t, N_HEADS, HEAD_DIM)
                                          for a in (q_sb, k_sb, v_sb, q_mb, k_mb, v_mb)]
    glu = a_val * jax.nn.sigmoid(a_gate)
    buf = jnp.concatenate([conv_buf, glu], axis=1)
    conv_out = depthwise_causal_conv(buf, w['conv_w'], w['conv_b'])
    new_conv = buf[:, buf.shape[1] - (CONV_WIDTH - 1):]
    y_a = jax.nn.silu(layer_norm(conv_out, w['conv_ln_g'], w['conv_ln_b']))
    if past is None:
        ksb, vsb, kmb, vmb = k_sb, v_sb, k_mb, v_mb
    else:
        ksb = jnp.concatenate([past[0], k_sb], axis=1)
        vsb = jnp.concatenate([past[1], v_sb], axis=1)
        kmb = jnp.concatenate([past[2], k_mb], axis=1)
        vmb = jnp.concatenate([past[3], v_mb], axis=1)
    tk = ksb.shape[1]
    k_pos = jnp.arange(tk, dtype=jnp.int32)
    q_pos = k_pos[tk - t:]
    y_b = sweep_queries(lambda qq, pp: stick_breaking(qq, ksb, vsb, pp, k_pos), q_sb, q_pos, SB_Q_BLOCK)
    kbh, vbh, kmean = moba_prepare(kmb, vmb)
    y_c = sweep_queries(lambda qq, pp: moba_query(qq, pp, kbh, vbh, kmean), q_mb, q_pos, MOBA_Q_BLOCK)
    branches = jnp.stack([y_a, y_b.reshape(n, t, D_BR), y_c.reshape(n, t, D_BR)], axis=2)
    y_br = jnp.einsum('ntbc,bcd->ntbd', branches, w['w_br'])
    gates = jax.nn.sigmoid(u @ w['w_gate'] + w['b_gate']).reshape(n, t, N_BRANCH, D_MODEL)
    h = h + jnp.sum(gates * y_br, axis=2) @ w['w_o']
    h = h + 0.5 * swiglu(rms_norm(h, w['n_ffn2']), w['ff2_in'], w['ff2_out'])
    h = h + jax.nn.sigmoid(rms_norm(h, w['n_ple']) @ w['w_pg']) * (p @ w['w_pe'])
    return h, (k_sb, v_sb, k_mb, v_mb, new_conv)


def setup_inputs(seed: int = 0) -> dict:
    key = jax.random.key(seed)
    ks = jax.random.split(key, 30)

    def nrm(k, shape, scale):
        return scale * jax.random.normal(k, shape, jnp.float32)

    def gain(k, shape):
        return 1.0 + nrm(k, shape, 0.02)

    n_pages = PAST_LEN // PAGE_SIZE
    n_pool = (DEC_BATCH * n_pages * 5) // 4
    cache_shape = (DEPTH, n_pool, PAGE_SIZE, N_HEADS, HEAD_DIM)
    page_table = jax.random.permutation(ks[9], n_pool)[:DEC_BATCH * n_pages]
    page_table = page_table.reshape(DEC_BATCH, n_pages).astype(jnp.int32)
    return {
        'x_prompt': nrm(ks[0], (BATCH, SEQ, D_MODEL), 1.0),
        'x_sample': nrm(ks[1], (DEC_BATCH, DEC_SEQ, D_MODEL), 1.0),
        'p_prompt': nrm(ks[2], (DEPTH, BATCH, SEQ, D_PLE), 1.0),
        'p_sample': nrm(ks[3], (DEPTH, DEC_BATCH, DEC_SEQ, D_PLE), 1.0),
        'cache_sb_k': nrm(ks[4], cache_shape, 1.0),
        'cache_sb_v': nrm(ks[5], cache_shape, 1.0),
        'cache_mb_k': nrm(ks[6], cache_shape, 1.0),
        'cache_mb_v': nrm(ks[7], cache_shape, 1.0),
        'state_conv': nrm(ks[8], (DEPTH, DEC_BATCH, CONV_WIDTH - 1, D_CONV), 0.5),
        'page_table': page_table,
        'norm_ffn1': gain(ks[10], (DEPTH, D_MODEL)),
        'w_ff1_in': nrm(ks[11], (DEPTH, D_MODEL, 2 * D_FF), D_MODEL ** -0.5),
        'w_ff1_out': nrm(ks[12], (DEPTH, D_FF, D_MODEL), D_FF ** -0.5),
        'norm_mix': gain(ks[13], (DEPTH, D_MODEL)),
        'w_in': nrm(ks[14], (DEPTH, D_MODEL, D_IN), D_MODEL ** -0.5),
        'conv_w': nrm(ks[15], (DEPTH, CONV_WIDTH, D_CONV), CONV_WIDTH ** -0.5),
        'conv_b': nrm(ks[16], (DEPTH, D_CONV), 0.02),
        'conv_ln_g': gain(ks[17], (DEPTH, D_CONV)),
        'conv_ln_b': nrm(ks[18], (DEPTH, D_CONV), 0.02),
        'w_br': nrm(ks[19], (DEPTH, N_BRANCH, D_BR, D_MODEL), D_BR ** -0.5),
        'w_gate': nrm(ks[20], (DEPTH, D_MODEL, N_BRANCH * D_MODEL), D_MODEL ** -0.5),
        'b_gate': nrm(ks[21], (DEPTH, N_BRANCH * D_MODEL), 0.02),
        'w_o': nrm(ks[22], (DEPTH, D_MODEL, D_MODEL), D_MODEL ** -0.5),
        'norm_ffn2': gain(ks[23], (DEPTH, D_MODEL)),
        'w_ff2_in': nrm(ks[24], (DEPTH, D_MODEL, 2 * D_FF), D_MODEL ** -0.5),
        'w_ff2_out': nrm(ks[25], (DEPTH, D_FF, D_MODEL), D_FF ** -0.5),
        'norm_ple': gain(ks[26], (DEPTH, D_MODEL)),
        'w_pg': nrm(ks[27], (DEPTH, D_MODEL, D_MODEL), D_MODEL ** -0.5),
        'w_pe': nrm(ks[28], (DEPTH, D_PLE, D_MODEL), D_PLE ** -0.5),
        'norm_final': gain(ks[29], (D_MODEL,)),
    }


def reference(x_prompt, x_sample, p_prompt, p_sample, cache_sb_k, cache_sb_v, cache_mb_k, cache_mb_v,
              state_conv, page_table, norm_ffn1, w_ff1_in, w_ff1_out, norm_mix, w_in, conv_w, conv_b,
              conv_ln_g, conv_ln_b, w_br, w_gate, b_gate, w_o, norm_ffn2, w_ff2_in, w_ff2_out,
              norm_ple, w_pg, w_pe, norm_final):
    hp, hs = x_prompt, x_sample
    rows_p, rows_s = [], []
    conv_zero = jnp.zeros((x_prompt.shape[0], CONV_WIDTH - 1, D_CONV), x_prompt.dtype)
    for i in range(DEPTH):
        w = {'n_ffn1': norm_ffn1[i], 'ff1_in': w_ff1_in[i], 'ff1_out': w_ff1_out[i],
             'n_mix': norm_mix[i], 'w_in': w_in[i], 'conv_w': conv_w[i], 'conv_b': conv_b[i],
             'conv_ln_g': conv_ln_g[i], 'conv_ln_b': conv_ln_b[i], 'w_br': w_br[i],
             'w_gate': w_gate[i], 'b_gate': b_gate[i], 'w_o': w_o[i], 'n_ffn2': norm_ffn2[i],
             'ff2_in': w_ff2_in[i], 'ff2_out': w_ff2_out[i], 'n_ple': norm_ple[i],
             'w_pg': w_pg[i], 'w_pe': w_pe[i]}
        hp, new_p = trunk_layer(hp, p_prompt[i], conv_zero, None, w)
        past = (gather_pages(cache_sb_k[i], page_table), gather_pages(cache_sb_v[i], page_table),
                gather_pages(cache_mb_k[i], page_table), gather_pages(cache_mb_v[i], page_table))
        hs, new_s = trunk_layer(hs, p_sample[i], state_conv[i], past, w)
        rows_p.append(new_p)
        rows_s.append(new_s)
    y_prompt = rms_norm(hp, norm_final)
    y_sample = rms_norm(hs, norm_final)
    sb_k_prompt = jnp.stack([r[0] for r in rows_p], axis=0)
    sb_v_prompt = jnp.stack([r[1] for r in rows_p], axis=0)
    mb_k_prompt = jnp.stack([r[2] for r in rows_p], axis=0)
    mb_v_prompt = jnp.stack([r[3] for r in rows_p], axis=0)
    conv_prompt = jnp.stack([r[4] for r in rows_p], axis=0)
    sb_k_sample = jnp.stack([r[0] for r in rows_s], axis=0)
    sb_v_sample = jnp.stack([r[1] for r in rows_s], axis=0)
    mb_k_sample = jnp.stack([r[2] for r in rows_s], axis=0)
    mb_v_sample = jnp.stack([r[3] for r in rows_s], axis=0)
    conv_sample = jnp.stack([r[4] for r in rows_s], axis=0)
    return (y_prompt, y_sample, sb_k_prompt, sb_v_prompt, mb_k_prompt, mb_v_prompt, conv_prompt,
            sb_k_sample, sb_v_sample, mb_k_sample, mb_v_sample, conv_sample)
```

```python
import functools

import jax
import jax.numpy as jnp
from jax import lax
from jax.experimental import pallas as pl
from jax.experimental.pallas import tpu as pltpu

F32 = jnp.float32
BF16 = jnp.bfloat16

RMS_EPS = 1e-6
LN_EPS = 1e-5
HEAD_DIM = 128
LANES = 128
VMEM_LIMIT = 56 * 1024 * 1024


def _cparams(sem):
    return pltpu.CompilerParams(dimension_semantics=sem, vmem_limit_bytes=VMEM_LIMIT)


def _rms(x, g):
    ms = jnp.mean(x * x, axis=-1, keepdims=True)
    return x * lax.rsqrt(ms + RMS_EPS) * g


def _sigmoid(x):
    return 1.0 / (1.0 + jnp.exp(-x))


def _ffn_kernel(x_ref, g_ref, wa0, wa1, wb0, wb1, wd0, wd1, o_ref, xn_ref, wup_ref, wdn_ref, *, n_chunks):
    f = pl.program_id(1)
    c = LANES

    @pl.when(f == 0)
    def _():
        xn_ref[...] = _rms(x_ref[...], g_ref[...]).astype(BF16)
        o_ref[...] = jnp.zeros_like(o_ref)

    second_valid = 2 * f + 1 < n_chunks
    wup_ref[:, 0 * c:1 * c] = wa0[...].astype(BF16)
    wup_ref[:, 1 * c:2 * c] = wa1[...].astype(BF16)
    wup_ref[:, 2 * c:3 * c] = wb0[...].astype(BF16)
    wup_ref[:, 3 * c:4 * c] = wb1[...].astype(BF16)
    wdn_ref[0:c, :] = wd0[...].astype(BF16)
    wdn_ref[c:2 * c, :] = jnp.where(second_valid, wd1[...], 0.0).astype(BF16)

    ab = jnp.dot(xn_ref[...], wup_ref[...], preferred_element_type=F32)
    a = ab[:, :2 * c]
    b = ab[:, 2 * c:]
    act = (a * _sigmoid(a) * b).astype(BF16)
    o_ref[...] += jnp.dot(act, wdn_ref[...], preferred_element_type=F32)

    @pl.when(f == pl.num_programs(1) - 1)
    def _():
        o_ref[...] = x_ref[...] + 0.5 * o_ref[...]


def ffn(x, g, w_up, w_down, *, layer, tm):
    m, d = x.shape
    d_ff = w_down.shape[1]
    c = LANES
    n_chunks = d_ff // c
    n_steps = (n_chunks + 1) // 2
    last = n_chunks - 1

    def up(off, k):
        return pl.BlockSpec((None, d, c), lambda i, f: (layer, 0, off + jnp.minimum(2 * f + k, last)))

    def down(k):
        return pl.BlockSpec((None, c, d), lambda i, f: (layer, jnp.minimum(2 * f + k, last), 0))

    row = pl.BlockSpec((tm, d), lambda i, f: (i, 0), pipeline_mode=pl.Buffered(1))
    return pl.pallas_call(
        functools.partial(_ffn_kernel, n_chunks=n_chunks),
        grid=(m // tm, n_steps),
        in_specs=[row, pl.BlockSpec((1, d), lambda i, f: (0, 0)),
                  up(0, 0), up(0, 1), up(n_chunks, 0), up(n_chunks, 1), down(0), down(1)],
        out_specs=pl.BlockSpec((tm, d), lambda i, f: (i, 0), pipeline_mode=pl.Buffered(1)),
        out_shape=jax.ShapeDtypeStruct((m, d), F32),
        scratch_shapes=[pltpu.VMEM((tm, d), BF16), pltpu.VMEM((d, 4 * c), BF16), pltpu.VMEM((2 * c, d), BF16)],
        compiler_params=_cparams(("parallel", "arbitrary")),
        name="ffn",
    )(x, g.reshape(1, d), w_up, w_up, w_up, w_up, w_down, w_down)


N_IN = 8


def _inproj_kernel(*refs, tn, n_alias):
    h_ref, g_ref = refs[:2]
    w_refs = refs[2:2 + N_IN]
    outs = refs[2 + N_IN + n_alias:2 + N_IN + n_alias + 7]
    u_ref, wcat_ref = refs[-2:]

    @pl.when(pl.program_id(1) == 0)
    def _():
        u_ref[...] = _rms(h_ref[...], g_ref[...]).astype(BF16)

    for c in range(N_IN):
        wcat_ref[:, c * tn:(c + 1) * tn] = w_refs[c][...].astype(BF16)
    y = jnp.dot(u_ref[...], wcat_ref[...], preferred_element_type=F32)
    outs[0][...] = y[:, :tn] * _sigmoid(y[:, tn:2 * tn])
    for c in range(2, N_IN):
        outs[c - 1][...] = y[:, c * tn:(c + 1) * tn].astype(outs[c - 1].dtype)


def inproj(h, g, w_in, *, layer, tm, tn=128, kv_bufs=None):
    m, d = h.shape
    depth = w_in.shape[0]
    d_br = w_in.shape[2] // N_IN
    nj = d_br // tn
    row = pl.BlockSpec((tm, d), lambda i, j: (i, 0), pipeline_mode=pl.Buffered(1))
    w_specs = [pl.BlockSpec((None, d, tn), functools.partial(lambda i, j, c: (layer, 0, c * nj + j), c=c))
               for c in range(N_IN)]
    flat = pl.BlockSpec((tm, tn), lambda i, j: (i, j))
    stacked = pl.BlockSpec((None, tm, tn), lambda i, j: (layer, i, j))
    flat_f32 = jax.ShapeDtypeStruct((m, d_br), F32)
    stacked_f32 = jax.ShapeDtypeStruct((depth, m, d_br), F32)
    out_specs = [flat, flat, stacked, stacked, flat, stacked, stacked]
    out_shape = [flat_f32, jax.ShapeDtypeStruct((m, d_br), BF16), stacked_f32, stacked_f32,
                 flat_f32, stacked_f32, stacked_f32]
    alias_in, aliases = [], {}
    if kv_bufs is not None:
        alias_in = list(kv_bufs)
        for n, o in enumerate((2, 3, 5, 6)):
            aliases[2 + N_IN + n] = o
    return pl.pallas_call(
        functools.partial(_inproj_kernel, tn=tn, n_alias=len(alias_in)),
        grid=(m // tm, nj),
        in_specs=[row, pl.BlockSpec((1, d), lambda i, j: (0, 0))] + w_specs
        + [pl.BlockSpec(memory_space=pl.ANY)] * len(alias_in),
        out_specs=out_specs,
        out_shape=out_shape,
        input_output_aliases=aliases,
        scratch_shapes=[pltpu.VMEM((tm, d), BF16), pltpu.VMEM((d, N_IN * tn), BF16)],
        compiler_params=_cparams(("parallel", "arbitrary")),
        name="inproj",
    )(h, g.reshape(1, d), *([w_in] * N_IN), *alias_in)


def _mix_kernel(h_ref, g_ref, ya, yb, yc, wg0, wg1, wg2, bg0, bg1, bg2, wbr, o_ref, u_ref):
    @pl.when(pl.program_id(1) == 0)
    def _():
        u_ref[...] = _rms(h_ref[...], g_ref[...]).astype(BF16)

    u = u_ref[...]
    acc = None
    for b, (y, wg, bg) in enumerate(((ya, wg0, bg0), (yb, wg1, bg1), (yc, wg2, bg2))):
        gate = _sigmoid(jnp.dot(u, wg[...].astype(BF16), preferred_element_type=F32) + bg[...])
        t = gate * jnp.dot(y[...], wbr[b].astype(BF16), preferred_element_type=F32)
        acc = t if acc is None else acc + t
    o_ref[...] = acc.astype(o_ref.dtype)


def mix(h, g, ya, yb, yc, w_gate, b_gate, w_br, *, layer, tm, tn=256):
    m, d = h.shape
    _, nb, d_br, _ = w_br.shape
    nj = d // tn
    row = pl.BlockSpec((tm, d), lambda i, j: (i, 0), pipeline_mode=pl.Buffered(1))
    br = pl.BlockSpec((tm, d_br), lambda i, j: (i, 0), pipeline_mode=pl.Buffered(1))
    wg = [pl.BlockSpec((None, d, tn), functools.partial(lambda i, j, b: (layer, 0, b * nj + j), b=b)) for b in range(nb)]
    bg = [pl.BlockSpec((1, tn), functools.partial(lambda i, j, b: (0, b * nj + j), b=b)) for b in range(nb)]
    b2 = b_gate.reshape(1, nb * d)
    return pl.pallas_call(
        _mix_kernel,
        grid=(m // tm, nj),
        in_specs=[row, pl.BlockSpec((1, d), lambda i, j: (0, 0)), br, br, br] + wg + bg
        + [pl.BlockSpec((None, nb, d_br, tn), lambda i, j: (layer, 0, 0, j))],
        out_specs=pl.BlockSpec((tm, tn), lambda i, j: (i, j)),
        out_shape=jax.ShapeDtypeStruct((m, d), BF16),
        scratch_shapes=[pltpu.VMEM((tm, d), BF16)],
        compiler_params=_cparams(("parallel", "arbitrary")),
        name="mix",
    )(h, g.reshape(1, d), ya, yb, yc, w_gate, w_gate, w_gate, b2, b2, b2, w_br)


def _matres_kernel(a_ref, w_ref, r_ref, o_ref):
    o_ref[...] = r_ref[...] + jnp.dot(a_ref[...], w_ref[...].astype(BF16), preferred_element_type=F32)


def matres(a, w, res, *, layer, tm, tn=256):
    m, k = a.shape
    n = w.shape[2]
    return pl.pallas_call(
        _matres_kernel,
        grid=(m // tm, n // tn),
        in_specs=[pl.BlockSpec((tm, k), lambda i, j: (i, 0)), pl.BlockSpec((None, k, tn), lambda i, j: (layer, 0, j)),
                  pl.BlockSpec((tm, tn), lambda i, j: (i, j))],
        out_specs=pl.BlockSpec((tm, tn), lambda i, j: (i, j)),
        out_shape=jax.ShapeDtypeStruct((m, n), F32),
        compiler_params=_cparams(("parallel", "arbitrary")),
        name="matres",
    )(a, w, res)


def _ple_kernel(h_ref, g_ref, p_ref, hj_ref, wpg, wpe, o_ref, hn_ref, pb_ref):
    @pl.when(pl.program_id(1) == 0)
    def _():
        hn_ref[...] = _rms(h_ref[...], g_ref[...]).astype(BF16)
        pb_ref[...] = p_ref[...].astype(BF16)

    gate = _sigmoid(jnp.dot(hn_ref[...], wpg[...].astype(BF16), preferred_element_type=F32))
    e = jnp.dot(pb_ref[...], wpe[...].astype(BF16), preferred_element_type=F32)
    o_ref[...] = hj_ref[...] + gate * e


def ple(h, g, p, w_pg, w_pe, *, layer, tm, tn=256):
    m, d = h.shape
    dp = p.shape[2]
    return pl.pallas_call(
        _ple_kernel,
        grid=(m // tm, d // tn),
        in_specs=[pl.BlockSpec((tm, d), lambda i, j: (i, 0), pipeline_mode=pl.Buffered(1)),
                  pl.BlockSpec((1, d), lambda i, j: (0, 0)),
                  pl.BlockSpec((None, tm, dp), lambda i, j: (layer, i, 0)),
                  pl.BlockSpec((tm, tn), lambda i, j: (i, j)),
                  pl.BlockSpec((None, d, tn), lambda i, j: (layer, 0, j)),
                  pl.BlockSpec((None, dp, tn), lambda i, j: (layer, 0, j))],
        out_specs=pl.BlockSpec((tm, tn), lambda i, j: (i, j)),
        out_shape=jax.ShapeDtypeStruct((m, d), F32),
        scratch_shapes=[pltpu.VMEM((tm, d), BF16), pltpu.VMEM((tm, dp), BF16)],
        compiler_params=_cparams(("parallel", "arbitrary")),
        name="ple",
    )(h, g.reshape(1, d), p, h, w_pg, w_pe)


def _rmsnorm_kernel(x_ref, g_ref, o_ref):
    o_ref[...] = _rms(x_ref[...], g_ref[...])


def rmsnorm(x, g, *, tm):
    m, d = x.shape
    return pl.pallas_call(
        _rmsnorm_kernel,
        grid=(m // tm,),
        in_specs=[pl.BlockSpec((tm, d), lambda i: (i, 0)), pl.BlockSpec((1, d), lambda i: (0, 0))],
        out_specs=pl.BlockSpec((tm, d), lambda i: (i, 0)),
        out_shape=jax.ShapeDtypeStruct((m, d), F32),
        compiler_params=_cparams(("parallel",)),
        name="final_norm",
    )(x, g.reshape(1, d))


CONV_HALO = 32
CONV_ROWS = 64


def _ln_swish(y, g, b):
    mu = jnp.mean(y, axis=-1, keepdims=True)
    yc = y - mu
    var = jnp.mean(yc * yc, axis=-1, keepdims=True)
    yn = yc * lax.rsqrt(var + LN_EPS) * g + b
    return yn * _sigmoid(yn)


def _conv_kernel(cur_ref, prev_ref, cw_ref, cb_ref, lg_ref, lb_ref, o_ref, buf_ref, co_ref, *, tt, width):
    t = pl.program_id(1)
    ch = cur_ref.shape[1]
    buf_ref[0:CONV_HALO, :] = jnp.where(t > 0, prev_ref[tt - CONV_HALO:tt, :], 0.0)
    buf_ref[CONV_HALO:, :] = cur_ref[...]
    off = CONV_HALO - (width - 1)
    for lc in range(ch // LANES):
        lanes = slice(lc * LANES, (lc + 1) * LANES)
        for tc in range(tt // CONV_ROWS):
            r0 = off + tc * CONV_ROWS
            acc = jnp.zeros((CONV_ROWS, LANES), F32) + cb_ref[:, lanes]
            for w in range(width):
                acc = acc + buf_ref[r0 + w:r0 + w + CONV_ROWS, lanes] * cw_ref[w:w + 1, lanes]
            co_ref[tc * CONV_ROWS:(tc + 1) * CONV_ROWS, lanes] = acc
    o_ref[...] = _ln_swish(co_ref[...], lg_ref[...], lb_ref[...]).astype(o_ref.dtype)


def conv_branch(glu, conv_w, conv_b, ln_g, ln_b, *, n_seq, tt=256):
    m, ch = glu.shape
    width = conv_w.shape[0]
    nt = m // n_seq // tt
    vec = pl.BlockSpec((1, ch), lambda b, t: (0, 0))
    return pl.pallas_call(
        functools.partial(_conv_kernel, tt=tt, width=width),
        grid=(n_seq, nt),
        in_specs=[pl.BlockSpec((tt, ch), lambda b, t: (b * nt + t, 0)),
                  pl.BlockSpec((tt, ch), lambda b, t: (jnp.maximum(b * nt + t - 1, 0), 0)),
                  pl.BlockSpec((width, ch), lambda b, t: (0, 0)), vec, vec, vec],
        out_specs=pl.BlockSpec((tt, ch), lambda b, t: (b * nt + t, 0)),
        out_shape=jax.ShapeDtypeStruct((m, ch), BF16),
        scratch_shapes=[pltpu.VMEM((tt + CONV_HALO, ch), F32), pltpu.VMEM((tt, ch), F32)],
        compiler_params=_cparams(("parallel", "arbitrary")),
        name="conv_branch",
    )(glu, glu, conv_w, conv_b.reshape(1, ch), ln_g.reshape(1, ch), ln_b.reshape(1, ch))


DEAD_LOG = -104.0


def _log_sigmoid(z):
    return jnp.minimum(z, 0.0) - jnp.log(1.0 + jnp.exp(-jnp.abs(z)))


def _split_bf16(x):
    hi = x.astype(BF16)
    return hi, (x - hi.astype(F32)).astype(BF16)


def _dot_nt(a, b):
    return lax.dot_general(a, b, (((1,), (1,)), ((), ())), preferred_element_type=F32)


def _sb_kernel(q_ref, k_ref, v_ref, o_ref, acc_ref, carry_ref, *, tq):
    i = pl.program_id(2)
    q = q_ref[...]
    row = lax.broadcasted_iota(jnp.int32, (tq, tq), 0)
    col = lax.broadcasted_iota(jnp.int32, (tq, tq), 1)
    after = jnp.where(row > col, 1.0, 0.0).astype(BF16)
    acc_ref[...] = jnp.zeros_like(acc_ref)
    carry_ref[...] = jnp.zeros_like(carry_ref)
    scale = HEAD_DIM ** -0.5

    def body(state):
        j, _ = state
        ks = pl.multiple_of(j * tq, tq)
        k = k_ref[pl.ds(ks, tq), :].astype(BF16)
        v = v_ref[pl.ds(ks, tq), :].astype(BF16)
        z = _dot_nt(q, k) * scale
        mask = (j * tq + col) < (i * tq + row)
        log_take = _log_sigmoid(z)
        log_stay = jnp.where(mask, log_take - z, 0.0)
        hi, lo = _split_bf16(log_stay)
        carry = carry_ref[...]
        later = carry + (jnp.dot(hi, after, preferred_element_type=F32) + jnp.dot(lo, after, preferred_element_type=F32))
        a = jnp.where(mask, jnp.exp(log_take + later), 0.0)
        acc_ref[...] += jnp.dot(a.astype(BF16), v, preferred_element_type=F32)
        carry = carry + jnp.sum(log_stay, axis=1, keepdims=True)
        carry_ref[...] = carry
        return j - 1, jnp.max(carry) > DEAD_LOG

    lax.while_loop(lambda s: jnp.logical_and(s[0] >= 0, s[1]), body, (i, True))
    o_ref[...] = acc_ref[...].astype(o_ref.dtype)


def sb_attention(q, k, v, *, n_seq, layer, tq=256):
    m, d_br = q.shape
    s = m // n_seq
    nq = s // tq
    kv = pl.BlockSpec((None, s, HEAD_DIM), lambda b, h, i: (layer, b, h))
    qo = pl.BlockSpec((tq, HEAD_DIM), lambda b, h, i: (b * nq + i, h))
    return pl.pallas_call(
        functools.partial(_sb_kernel, tq=tq),
        grid=(n_seq, d_br // HEAD_DIM, nq),
        in_specs=[qo, kv, kv],
        out_specs=qo,
        out_shape=jax.ShapeDtypeStruct((m, d_br), BF16),
        scratch_shapes=[pltpu.VMEM((tq, HEAD_DIM), F32), pltpu.VMEM((tq, 1), F32)],
        compiler_params=_cparams(("parallel", "parallel", "arbitrary")),
        name="sb_attention",
    )(q, k, v)


MOBA_BLOCK = 256
MOBA_TOPK = 3
NEG_BIG = -1e30


def _dot3(a, b_hi, b_lo):
    a_hi, a_lo = _split_bf16(a)
    return _dot_nt(a_hi, b_hi) + (_dot_nt(a_hi, b_lo) + _dot_nt(a_lo, b_hi))


def _moba_kernel(q_ref, k_ref, v_ref, o_ref, kmean_ref, *, n_blocks):
    i = pl.program_id(2)
    blk = MOBA_BLOCK
    scale = HEAD_DIM ** -0.5

    @pl.when(i == 0)
    def _():
        kmean_ref[...] = jnp.zeros_like(kmean_ref)
        for n in range(n_blocks):
            kmean_ref[n:n + 1, :] = jnp.mean(k_ref[n * blk:(n + 1) * blk, :], axis=0, keepdims=True)

    qf = q_ref[...]
    q = qf.astype(BF16)
    km_hi, km_lo = _split_bf16(kmean_ref[...])
    gate = _dot3(qf, km_hi, km_lo)
    lane = lax.broadcasted_iota(jnp.int32, gate.shape, 1)
    past = lane < i

    row = lax.broadcasted_iota(jnp.int32, (blk, blk), 0)
    col = lax.broadcasted_iota(jnp.int32, (blk, blk), 1)
    own = pl.multiple_of(i * blk, blk)
    s = jnp.where(col <= row, _dot_nt(q, k_ref[pl.ds(own, blk), :].astype(BF16)) * scale, NEG_BIG)
    m0 = jnp.max(s, axis=1, keepdims=True)
    p = jnp.exp(s - m0)
    l0 = jnp.sum(p, axis=1, keepdims=True)
    acc0 = jnp.dot(p.astype(BF16), v_ref[pl.ds(own, blk), :].astype(BF16), preferred_element_type=F32)

    def body(j, state):
        m, l, acc = state
        gj = jnp.sum(jnp.where(lane == j, gate, 0.0), axis=1, keepdims=True)
        ahead = jnp.logical_and(past, jnp.logical_or(gate > gj, jnp.logical_and(gate == gj, lane < j)))
        rank = jnp.sum(jnp.where(ahead, 1.0, 0.0), axis=1, keepdims=True)
        ks = pl.multiple_of(j * blk, blk)
        s = _dot_nt(q, k_ref[pl.ds(ks, blk), :].astype(BF16)) * scale
        s = jnp.where(rank < MOBA_TOPK, s, NEG_BIG)
        m_new = jnp.maximum(m, jnp.max(s, axis=1, keepdims=True))
        alpha = jnp.exp(m - m_new)
        p = jnp.exp(s - m_new)
        l = alpha * l + jnp.sum(p, axis=1, keepdims=True)
        acc = alpha * acc + jnp.dot(p.astype(BF16), v_ref[pl.ds(ks, blk), :].astype(BF16), preferred_element_type=F32)
        return m_new, l, acc

    m, l, acc = lax.fori_loop(0, i, body, (m0, l0, acc0))
    o_ref[...] = (acc / l).astype(o_ref.dtype)


def moba_attention(q, k, v, *, n_seq, layer):
    m, d_br = q.shape
    s = m // n_seq
    nq = s // MOBA_BLOCK
    kv = pl.BlockSpec((None, s, HEAD_DIM), lambda b, h, i: (layer, b, h))
    qo = pl.BlockSpec((MOBA_BLOCK, HEAD_DIM), lambda b, h, i: (b * nq + i, h))
    return pl.pallas_call(
        functools.partial(_moba_kernel, n_blocks=nq),
        grid=(n_seq, d_br // HEAD_DIM, nq),
        in_specs=[qo, kv, kv],
        out_specs=qo,
        out_shape=jax.ShapeDtypeStruct((m, d_br), BF16),
        scratch_shapes=[pltpu.VMEM((LANES, HEAD_DIM), F32)],
        compiler_params=_cparams(("parallel", "parallel", "arbitrary")),
        name="moba_attention",
    )(q, k, v)


def _sb_step_kernel(pt_ref, q_ref, kc_ref, vc_ref, o_ref, kbuf, vbuf, sem, acc_ref, carry_ref, *, layer, n_pages):
    b = pl.program_id(0)
    page_len, n_heads, d = kbuf.shape[1:]
    flat = page_len * n_heads
    n_chunks = flat // LANES
    scale = HEAD_DIM ** -0.5

    def copies(p, slot):
        page = pt_ref[b * n_pages + p]
        return (pltpu.make_async_copy(kc_ref.at[layer, page], kbuf.at[slot], sem.at[0, slot]),
                pltpu.make_async_copy(vc_ref.at[layer, page], vbuf.at[slot], sem.at[1, slot]))

    def start(p, slot):
        for c in copies(p, slot):
            c.start()

    def wait(p, slot):
        for c in copies(p, slot):
            c.wait()

    q = q_ref[...].astype(BF16)
    sub = lax.broadcasted_iota(jnp.int32, (n_heads, flat), 0)
    lane = lax.broadcasted_iota(jnp.int32, (n_heads, flat), 1)
    own_head = (lane % n_heads) == sub
    r2 = lax.broadcasted_iota(jnp.int32, (LANES, LANES), 0)
    c2 = lax.broadcasted_iota(jnp.int32, (LANES, LANES), 1)
    after = jnp.where(r2 > c2, 1.0, 0.0).astype(BF16)
    acc_ref[...] = jnp.zeros_like(acc_ref)
    carry_ref[...] = jnp.zeros_like(carry_ref)
    start(n_pages - 1, 0)

    def body(state):
        p, _ = state
        slot = (n_pages - 1 - p) % 2

        @pl.when(p > 0)
        def _():
            start(p - 1, 1 - slot)

        wait(p, slot)
        kf = kbuf[slot].reshape(flat, d).astype(BF16)
        vf = vbuf[slot].reshape(flat, d).astype(BF16)
        z = _dot_nt(q, kf) * scale
        log_take = _log_sigmoid(z)
        log_stay = jnp.where(own_head, log_take - z, 0.0)
        stacked = jnp.concatenate([log_stay[:, c * LANES:(c + 1) * LANES] for c in range(n_chunks)], axis=0)
        hi, lo = _split_bf16(stacked)
        within = jnp.dot(hi, after, preferred_element_type=F32) + jnp.dot(lo, after, preferred_element_type=F32)
        total = jnp.sum(stacked, axis=1, keepdims=True)
        run = carry_ref[...]
        pieces = [None] * n_chunks
        for c in range(n_chunks - 1, -1, -1):
            rows = slice(c * n_heads, (c + 1) * n_heads)
            lanes = slice(c * LANES, (c + 1) * LANES)
            later = run + within[rows, :]
            pieces[c] = jnp.where(own_head[:, lanes], jnp.exp(log_take[:, lanes] + later), 0.0)
            run = run + total[rows, :]
        a = jnp.concatenate(pieces, axis=1).astype(BF16)
        acc_ref[...] += jnp.dot(a, vf, preferred_element_type=F32)
        carry_ref[...] = run
        return p - 1, jnp.max(run) > DEAD_LOG

    p_end, _ = lax.while_loop(lambda s: jnp.logical_and(s[0] >= 0, s[1]), body, (n_pages - 1, True))

    @pl.when(p_end >= 0)
    def _():
        wait(p_end, (n_pages - 1 - p_end) % 2)

    o_ref[...] = acc_ref[...]


def sb_step(q, k_cache, v_cache, page_table, *, layer):
    n_seq, n_heads, d = q.shape
    n_pages = page_table.shape[1]
    page_len = k_cache.shape[2]
    tok = pl.BlockSpec((None, n_heads, d), lambda b, pt: (b, 0, 0))
    return pl.pallas_call(
        functools.partial(_sb_step_kernel, layer=layer, n_pages=n_pages),
        grid_spec=pltpu.PrefetchScalarGridSpec(
            num_scalar_prefetch=1,
            grid=(n_seq,),
            in_specs=[tok, pl.BlockSpec(memory_space=pl.ANY), pl.BlockSpec(memory_space=pl.ANY)],
            out_specs=tok,
            scratch_shapes=[pltpu.VMEM((2, page_len, n_heads, d), F32), pltpu.VMEM((2, page_len, n_heads, d), F32),
                            pltpu.SemaphoreType.DMA((2, 2)),
                            pltpu.VMEM((n_heads, d), F32), pltpu.VMEM((n_heads, 1), F32)]),
        out_shape=jax.ShapeDtypeStruct((n_seq, n_heads, d), F32),
        compiler_params=_cparams(("arbitrary",)),
        name="sb_step",
    )(page_table.reshape(-1), q, k_cache, v_cache)


GATE_PAGES = 8


def _moba_gate_kernel(pt_ref, q_ref, *refs, pages_per_block, n_blocks):
    pages = refs[:GATE_PAGES]
    idx_ref, g_ref = refs[GATE_PAGES:]
    s = pl.program_id(1)
    per_step = GATE_PAGES // pages_per_block
    block_len = pages_per_block * pages[0].shape[0]

    @pl.when(s == 0)
    def _():
        g_ref[...] = jnp.full_like(g_ref, -jnp.inf)

    q = q_ref[...]
    lane = lax.broadcasted_iota(jnp.int32, g_ref.shape, 1)
    g = g_ref[...]
    for n in range(per_step):
        ksum = jnp.sum(pages[n * pages_per_block][...], axis=0)
        for r in range(1, pages_per_block):
            ksum = ksum + jnp.sum(pages[n * pages_per_block + r][...], axis=0)
        gn = jnp.sum(ksum * q, axis=1, keepdims=True) * (1.0 / block_len)
        g = jnp.where(lane == s * per_step + n, gn, g)
    g_ref[...] = g

    @pl.when(s == pl.num_programs(1) - 1)
    def _():
        lane_f = lane.astype(F32)
        gg = g
        out = jnp.zeros(g_ref.shape, jnp.int32)
        for r in range(MOBA_TOPK):
            best = jnp.max(gg, axis=1, keepdims=True)
            first = jnp.min(jnp.where(gg == best, lane_f, float(LANES)), axis=1, keepdims=True)
            out = jnp.where(lane == r, first.astype(jnp.int32), out)
            gg = jnp.where(lane_f == first, -jnp.inf, gg)
        idx_ref[...] = out


def moba_gate(q, k_cache, page_table, *, layer):
    n_seq, n_heads, d = q.shape
    n_pages = page_table.shape[1]
    page_len = k_cache.shape[2]
    pages_per_block = MOBA_BLOCK // page_len
    n_blocks = n_pages // pages_per_block
    assert MOBA_TOPK <= n_blocks <= LANES and n_pages % GATE_PAGES == 0 and GATE_PAGES % pages_per_block == 0
    tok = pl.BlockSpec((None, n_heads, d), lambda b, s, pt: (b, 0, 0))
    page_specs = [
        pl.BlockSpec((None, None, page_len, n_heads, d),
                     functools.partial(lambda b, s, pt, r: (layer, pt[b * n_pages + s * GATE_PAGES + r], 0, 0, 0), r=r))
        for r in range(GATE_PAGES)]
    return pl.pallas_call(
        functools.partial(_moba_gate_kernel, pages_per_block=pages_per_block, n_blocks=n_blocks),
        grid_spec=pltpu.PrefetchScalarGridSpec(
            num_scalar_prefetch=1,
            grid=(n_seq, n_pages // GATE_PAGES),
            in_specs=[tok] + page_specs,
            out_specs=pl.BlockSpec((None, n_heads, LANES), lambda b, s, pt: (b, 0, 0)),
            scratch_shapes=[pltpu.VMEM((n_heads, LANES), F32)]),
        out_shape=jax.ShapeDtypeStruct((n_seq, n_heads, LANES), jnp.int32),
        compiler_params=_cparams(("parallel", "arbitrary")),
        name="moba_gate",
    )(page_table.reshape(-1), q, *([k_cache] * GATE_PAGES))


def _moba_step_kernel(pt_ref, sel_ref, q_ref, kn_ref, vn_ref, kc_ref, vc_ref, o_ref, kg, vg, sem, *,
                      layer, n_pages, pages_per_block):
    b = pl.program_id(0)
    n_heads, d = q_ref.shape
    page_len = kg.shape[1] // (MOBA_TOPK * pages_per_block)
    scale = HEAD_DIM ** -0.5

    def copies():
        out = []
        for h in range(n_heads):
            for r in range(MOBA_TOPK):
                blk = sel_ref[(b * n_heads + h) * MOBA_TOPK + r]
                for g in range(pages_per_block):
                    page = pt_ref[b * n_pages + blk * pages_per_block + g]
                    rows = pl.ds((r * pages_per_block + g) * page_len, page_len)
                    out.append(pltpu.make_async_copy(kc_ref.at[layer, page, :, h, :], kg.at[h, rows, :], sem.at[0]))
                    out.append(pltpu.make_async_copy(vc_ref.at[layer, page, :, h, :], vg.at[h, rows, :], sem.at[1]))
        return out

    cps = copies()
    for c in cps:
        c.start()
    for c in cps:
        c.wait()

    qf = q_ref[...]
    q = qf.astype(BF16)
    sub = lax.broadcasted_iota(jnp.int32, (n_heads, d), 0)
    s_own = jnp.sum(qf * kn_ref[...], axis=1, keepdims=True) * scale
    res = jnp.zeros((n_heads, d), F32)
    for h in range(n_heads):
        s = _dot_nt(q, kg[h].astype(BF16)) * scale
        m = jnp.maximum(jnp.max(s, axis=1, keepdims=True), s_own)
        p = jnp.exp(s - m)
        p_own = jnp.exp(s_own - m)
        l = jnp.sum(p, axis=1, keepdims=True) + p_own
        o = (jnp.dot(p.astype(BF16), vg[h].astype(BF16), preferred_element_type=F32) + p_own * vn_ref[...]) / l
        res = jnp.where(sub == h, o, res)
    o_ref[...] = res


def moba_step(q, k_new, v_new, sel, k_cache, v_cache, page_table, *, layer):
    n_seq, n_heads, d = q.shape
    n_pages = page_table.shape[1]
    page_len = k_cache.shape[2]
    pages_per_block = MOBA_BLOCK // page_len
    tok = pl.BlockSpec((None, n_heads, d), lambda b, pt, sl: (b, 0, 0))
    gathered = pltpu.VMEM((n_heads, MOBA_TOPK * MOBA_BLOCK, d), F32)
    return pl.pallas_call(
        functools.partial(_moba_step_kernel, layer=layer, n_pages=n_pages, pages_per_block=pages_per_block),
        grid_spec=pltpu.PrefetchScalarGridSpec(
            num_scalar_prefetch=2,
            grid=(n_seq,),
            in_specs=[tok, tok, tok, pl.BlockSpec(memory_space=pl.ANY), pl.BlockSpec(memory_space=pl.ANY)],
            out_specs=tok,
            scratch_shapes=[gathered, gathered, pltpu.SemaphoreType.DMA((2,))]),
        out_shape=jax.ShapeDtypeStruct((n_seq, n_heads, d), F32),
        compiler_params=_cparams(("arbitrary",)),
        name="moba_step",
    )(page_table.reshape(-1), sel[:, :, :MOBA_TOPK].reshape(-1), q, k_new, v_new, k_cache, v_cache)


def _conv_step_kernel(st_ref, glu_ref, cw_ref, cb_ref, lg_ref, lb_ref, y_ref, ns_ref, *, width):
    n_seq = st_ref.shape[0]
    past = width - 1
    rows = []
    for b in range(n_seq):
        st = st_ref[b]
        new = glu_ref[b:b + 1, :]
        co = jnp.sum(st * cw_ref[0:past, :], axis=0, keepdims=True) + new * cw_ref[past:width, :] + cb_ref[...]
        rows.append(_ln_swish(co, lg_ref[...], lb_ref[...]))
        ns_ref[b, 0:past - 1, :] = st[1:past]
        ns_ref[b, past - 1:past, :] = new
    y_ref[...] = jnp.concatenate(rows, axis=0).astype(y_ref.dtype)


def conv_step(state, glu, conv_w, conv_b, ln_g, ln_b):
    n_seq, past, ch = state.shape
    width = conv_w.shape[0]
    return pl.pallas_call(
        functools.partial(_conv_step_kernel, width=width),
        out_shape=[jax.ShapeDtypeStruct((n_seq, ch), BF16), jax.ShapeDtypeStruct((n_seq, past, ch), F32)],
        compiler_params=pltpu.CompilerParams(vmem_limit_bytes=VMEM_LIMIT),
        name="conv_step",
    )(state, glu, conv_w, conv_b.reshape(1, ch), ln_g.reshape(1, ch), ln_b.reshape(1, ch))


PROMPT_ROWS = 1024


def kernel(x_prompt, x_sample, p_prompt, p_sample, cache_sb_k, cache_sb_v, cache_mb_k, cache_mb_v, state_conv,
           page_table, norm_ffn1, w_ff1_in, w_ff1_out, norm_mix, w_in, conv_w, conv_b, conv_ln_g, conv_ln_b, w_br,
           w_gate, b_gate, w_o, norm_ffn2, w_ff2_in, w_ff2_out, norm_ple, w_pg, w_pe, norm_final):
    depth = w_in.shape[0]
    n_seq, seq, d = x_prompt.shape
    n_dec, dec_seq, _ = x_sample.shape
    past = state_conv.shape[2]
    d_br = w_br.shape[2]
    n_heads = d_br // HEAD_DIM
    assert dec_seq == 1 and seq >= past and seq % MOBA_BLOCK == 0
    m = n_seq * seq
    tm = min(PROMPT_ROWS, m)

    def trunk_layer(layer, h, p, rows, kv_bufs, branches):
        h = ffn(h, norm_ffn1[layer], w_ff1_in, w_ff1_out, layer=layer, tm=rows)
        glu, q_sb, k_sb, v_sb, q_mb, k_mb, v_mb = inproj(h, norm_mix[layer], w_in, layer=layer, tm=rows, kv_bufs=kv_bufs)
        kv_bufs = (k_sb, v_sb, k_mb, v_mb)
        y_a, y_b, y_c, aux = branches(glu, q_sb, q_mb, kv_bufs)
        mixed = mix(h, norm_mix[layer], y_a, y_b, y_c, w_gate, b_gate[layer], w_br, layer=layer, tm=rows)
        h = matres(mixed, w_o, h, layer=layer, tm=rows)
        h = ffn(h, norm_ffn2[layer], w_ff2_in, w_ff2_out, layer=layer, tm=rows)
        h = ple(h, norm_ple[layer], p, w_pg, w_pe, layer=layer, tm=rows)
        return h, kv_bufs, aux

    hp = x_prompt.reshape(m, d)
    hs = x_sample.reshape(n_dec, d)
    pp = p_prompt.reshape(depth, m, -1)
    ps = p_sample.reshape(depth, n_dec, -1)
    kv_p = (jnp.zeros((depth, m, d_br), F32),) * 4
    kv_s = (jnp.zeros((depth, n_dec, d_br), F32),) * 4
    conv_p, conv_s = [], []
    for layer in range(depth):
        conv_args = (conv_w[layer], conv_b[layer], conv_ln_g[layer], conv_ln_b[layer])

        def prompt_branches(glu, q_sb, q_mb, kv, layer=layer, conv_args=conv_args):
            y_a = conv_branch(glu, *conv_args, n_seq=n_seq)
            y_b = sb_attention(q_sb, kv[0], kv[1], n_seq=n_seq, layer=layer)
            y_c = moba_attention(q_mb, kv[2], kv[3], n_seq=n_seq, layer=layer)
            return y_a, y_b, y_c, glu.reshape(n_seq, seq, d_br)[:, seq - past:]

        def sample_branches(glu, q_sb, q_mb, kv, layer=layer, conv_args=conv_args):
            y_a, new_state = conv_step(state_conv[layer], glu, *conv_args)
            tok = (n_dec, n_heads, HEAD_DIM)
            y_b = sb_step(q_sb.astype(F32).reshape(tok), cache_sb_k, cache_sb_v, page_table, layer=layer)
            q3 = q_mb.reshape(tok)
            sel = moba_gate(q3, cache_mb_k, page_table, layer=layer)
            y_c = moba_step(q3, kv[2][layer].reshape(tok), kv[3][layer].reshape(tok), sel, cache_mb_k, cache_mb_v,
                            page_table, layer=layer)
            return y_a, y_b.reshape(n_dec, d_br).astype(BF16), y_c.reshape(n_dec, d_br).astype(BF16), new_state

        hp, kv_p, aux_p = trunk_layer(layer, hp, pp, tm, kv_p, prompt_branches)
        hs, kv_s, aux_s = trunk_layer(layer, hs, ps, n_dec, kv_s, sample_branches)
        conv_p.append(aux_p)
        conv_s.append(aux_s)

    y_prompt = rmsnorm(hp, norm_final, tm=tm).reshape(n_seq, seq, d)
    y_sample = rmsnorm(hs, norm_final, tm=n_dec).reshape(n_dec, 1, d)
    rows_p = [a.reshape(depth, n_seq, seq, n_heads, HEAD_DIM) for a in kv_p]
    rows_s = [a.reshape(depth, n_dec, 1, n_heads, HEAD_DIM) for a in kv_s]
    return (y_prompt, y_sample, *rows_p, jnp.stack(conv_p), *rows_s, jnp.stack(conv_s))
```

```python
import functools

import jax
import jax.numpy as jnp
from jax import lax
from jax.experimental import pallas as pl
from jax.experimental.pallas import tpu as pltpu

F32 = jnp.float32
BF16 = jnp.bfloat16

RMS_EPS = 1e-6
LN_EPS = 1e-5
HEAD_DIM = 128
LANES = 128
VMEM_LIMIT = 56 * 1024 * 1024


def _cparams(sem):
    return pltpu.CompilerParams(dimension_semantics=sem, vmem_limit_bytes=VMEM_LIMIT)


def _rms(x, g):
    ms = jnp.mean(x * x, axis=-1, keepdims=True)
    return x * lax.rsqrt(ms + RMS_EPS) * g


def _sigmoid(x):
    return 1.0 / (1.0 + jnp.exp(-x))


FFN_CHUNKS = 4


def _ffn_kernel(*refs, n_chunks):
    nc = FFN_CHUNKS
    x_ref, g_ref = refs[:2]
    wa, wb, wd = refs[2:2 + nc], refs[2 + nc:2 + 2 * nc], refs[2 + 2 * nc:2 + 3 * nc]
    o_ref, xn_ref, wup_ref, wdn_ref = refs[2 + 3 * nc:]
    f = pl.program_id(1)
    c = LANES

    @pl.when(f == 0)
    def _():
        xn_ref[...] = _rms(x_ref[...], g_ref[...]).astype(BF16)
        o_ref[...] = jnp.zeros_like(o_ref)

    for k in range(nc):
        wup_ref[:, k * c:(k + 1) * c] = wa[k][...].astype(BF16)
        wup_ref[:, (nc + k) * c:(nc + k + 1) * c] = wb[k][...].astype(BF16)
        wdk = wd[k][...] if k == 0 else jnp.where(nc * f + k < n_chunks, wd[k][...], 0.0)
        wdn_ref[k * c:(k + 1) * c, :] = wdk.astype(BF16)

    ab = jnp.dot(xn_ref[...], wup_ref[...], preferred_element_type=F32)
    a = ab[:, :nc * c]
    b = ab[:, nc * c:]
    act = (a * _sigmoid(a) * b).astype(BF16)
    o_ref[...] += jnp.dot(act, wdn_ref[...], preferred_element_type=F32)

    @pl.when(f == pl.num_programs(1) - 1)
    def _():
        o_ref[...] = x_ref[...] + 0.5 * o_ref[...]


def ffn(x, g, w_up, w_down, *, layer, tm):
    m, d = x.shape
    d_ff = w_down.shape[1]
    c = LANES
    nc = FFN_CHUNKS
    n_chunks = d_ff // c
    n_steps = -(-n_chunks // nc)
    last = n_chunks - 1

    def up(off, k):
        return pl.BlockSpec((None, d, c), lambda i, f: (layer, 0, off + jnp.minimum(nc * f + k, last)))

    def down(k):
        return pl.BlockSpec((None, c, d), lambda i, f: (layer, jnp.minimum(nc * f + k, last), 0))

    row = pl.BlockSpec((tm, d), lambda i, f: (i, 0), pipeline_mode=pl.Buffered(1))
    return pl.pallas_call(
        functools.partial(_ffn_kernel, n_chunks=n_chunks),
        grid=(m // tm, n_steps),
        in_specs=[row, pl.BlockSpec((1, d), lambda i, f: (0, 0))]
        + [up(0, k) for k in range(nc)] + [up(n_chunks, k) for k in range(nc)] + [down(k) for k in range(nc)],
        out_specs=pl.BlockSpec((tm, d), lambda i, f: (i, 0), pipeline_mode=pl.Buffered(1)),
        out_shape=jax.ShapeDtypeStruct((m, d), F32),
        scratch_shapes=[pltpu.VMEM((tm, d), BF16), pltpu.VMEM((d, 2 * nc * c), BF16), pltpu.VMEM((nc * c, d), BF16)],
        compiler_params=_cparams(("parallel", "arbitrary")),
        name="ffn",
    )(x, g.reshape(1, d), *([w_up] * (2 * nc)), *([w_down] * nc))


N_IN = 8


def _inproj_kernel(*refs, tn, n_alias):
    h_ref, g_ref = refs[:2]
    w_refs = refs[2:2 + N_IN]
    outs = refs[2 + N_IN + n_alias:2 + N_IN + n_alias + 7]
    u_ref, wcat_ref = refs[-2:]

    @pl.when(pl.program_id(1) == 0)
    def _():
        u_ref[...] = _rms(h_ref[...], g_ref[...]).astype(BF16)

    for c in range(N_IN):
        wcat_ref[:, c * tn:(c + 1) * tn] = w_refs[c][...].astype(BF16)
    y = jnp.dot(u_ref[...], wcat_ref[...], preferred_element_type=F32)
    outs[0][...] = y[:, :tn] * _sigmoid(y[:, tn:2 * tn])
    for c in range(2, N_IN):
        outs[c - 1][...] = y[:, c * tn:(c + 1) * tn].astype(outs[c - 1].dtype)


def inproj(h, g, w_in, *, layer, tm, tn=128, kv_bufs=None):
    m, d = h.shape
    depth = w_in.shape[0]
    d_br = w_in.shape[2] // N_IN
    nj = d_br // tn
    row = pl.BlockSpec((tm, d), lambda i, j: (i, 0), pipeline_mode=pl.Buffered(1))
    w_specs = [pl.BlockSpec((None, d, tn), functools.partial(lambda i, j, c: (layer, 0, c * nj + j), c=c))
               for c in range(N_IN)]
    flat = pl.BlockSpec((tm, tn), lambda i, j: (i, j))
    stacked = pl.BlockSpec((None, tm, tn), lambda i, j: (layer, i, j))
    flat_f32 = jax.ShapeDtypeStruct((m, d_br), F32)
    stacked_f32 = jax.ShapeDtypeStruct((depth, m, d_br), F32)
    out_specs = [flat, flat, stacked, stacked, flat, stacked, stacked]
    out_shape = [flat_f32, jax.ShapeDtypeStruct((m, d_br), BF16), stacked_f32, stacked_f32,
                 flat_f32, stacked_f32, stacked_f32]
    alias_in, aliases = [], {}
    if kv_bufs is not None:
        alias_in = list(kv_bufs)
        for n, o in enumerate((2, 3, 5, 6)):
            aliases[2 + N_IN + n] = o
    return pl.pallas_call(
        functools.partial(_inproj_kernel, tn=tn, n_alias=len(alias_in)),
        grid=(m // tm, nj),
        in_specs=[row, pl.BlockSpec((1, d), lambda i, j: (0, 0))] + w_specs
        + [pl.BlockSpec(memory_space=pl.ANY)] * len(alias_in),
        out_specs=out_specs,
        out_shape=out_shape,
        input_output_aliases=aliases,
        scratch_shapes=[pltpu.VMEM((tm, d), BF16), pltpu.VMEM((d, N_IN * tn), BF16)],
        compiler_params=_cparams(("parallel", "arbitrary")),
        name="inproj",
    )(h, g.reshape(1, d), *([w_in] * N_IN), *alias_in)


def _mix_kernel(h_ref, g_ref, ya, yb, yc, wg0, wg1, wg2, bg0, bg1, bg2, wbr, o_ref, u_ref):
    @pl.when(pl.program_id(1) == 0)
    def _():
        u_ref[...] = _rms(h_ref[...], g_ref[...]).astype(BF16)

    u = u_ref[...]
    acc = None
    for b, (y, wg, bg) in enumerate(((ya, wg0, bg0), (yb, wg1, bg1), (yc, wg2, bg2))):
        gate = _sigmoid(jnp.dot(u, wg[...].astype(BF16), preferred_element_type=F32) + bg[...])
        t = gate * jnp.dot(y[...], wbr[b].astype(BF16), preferred_element_type=F32)
        acc = t if acc is None else acc + t
    o_ref[...] = acc.astype(o_ref.dtype)


def mix(h, g, ya, yb, yc, w_gate, b_gate, w_br, *, layer, tm, tn=256):
    m, d = h.shape
    _, nb, d_br, _ = w_br.shape
    nj = d // tn
    row = pl.BlockSpec((tm, d), lambda i, j: (i, 0), pipeline_mode=pl.Buffered(1))
    br = pl.BlockSpec((tm, d_br), lambda i, j: (i, 0), pipeline_mode=pl.Buffered(1))
    wg = [pl.BlockSpec((None, d, tn), functools.partial(lambda i, j, b: (layer, 0, b * nj + j), b=b)) for b in range(nb)]
    bg = [pl.BlockSpec((1, tn), functools.partial(lambda i, j, b: (0, b * nj + j), b=b)) for b in range(nb)]
    b2 = b_gate.reshape(1, nb * d)
    return pl.pallas_call(
        _mix_kernel,
        grid=(m // tm, nj),
        in_specs=[row, pl.BlockSpec((1, d), lambda i, j: (0, 0)), br, br, br] + wg + bg
        + [pl.BlockSpec((None, nb, d_br, tn), lambda i, j: (layer, 0, 0, j))],
        out_specs=pl.BlockSpec((tm, tn), lambda i, j: (i, j)),
        out_shape=jax.ShapeDtypeStruct((m, d), BF16),
        scratch_shapes=[pltpu.VMEM((tm, d), BF16)],
        compiler_params=_cparams(("parallel", "arbitrary")),
        name="mix",
    )(h, g.reshape(1, d), ya, yb, yc, w_gate, w_gate, w_gate, b2, b2, b2, w_br)


def _matres_kernel(a_ref, w_ref, r_ref, o_ref):
    o_ref[...] = r_ref[...] + jnp.dot(a_ref[...], w_ref[...].astype(BF16), preferred_element_type=F32)


def matres(a, w, res, *, layer, tm, tn=256):
    m, k = a.shape
    n = w.shape[2]
    return pl.pallas_call(
        _matres_kernel,
        grid=(m // tm, n // tn),
        in_specs=[pl.BlockSpec((tm, k), lambda i, j: (i, 0)), pl.BlockSpec((None, k, tn), lambda i, j: (layer, 0, j)),
                  pl.BlockSpec((tm, tn), lambda i, j: (i, j))],
        out_specs=pl.BlockSpec((tm, tn), lambda i, j: (i, j)),
        out_shape=jax.ShapeDtypeStruct((m, n), F32),
        compiler_params=_cparams(("parallel", "arbitrary")),
        name="matres",
    )(a, w, res)


def _ple_kernel(h_ref, g_ref, p_ref, hj_ref, wpg, wpe, o_ref, hn_ref, pb_ref):
    @pl.when(pl.program_id(1) == 0)
    def _():
        hn_ref[...] = _rms(h_ref[...], g_ref[...]).astype(BF16)
        pb_ref[...] = p_ref[...].astype(BF16)

    gate = _sigmoid(jnp.dot(hn_ref[...], wpg[...].astype(BF16), preferred_element_type=F32))
    e = jnp.dot(pb_ref[...], wpe[...].astype(BF16), preferred_element_type=F32)
    o_ref[...] = hj_ref[...] + gate * e


def ple(h, g, p, w_pg, w_pe, *, layer, tm, tn=256):
    m, d = h.shape
    dp = p.shape[2]
    return pl.pallas_call(
        _ple_kernel,
        grid=(m // tm, d // tn),
        in_specs=[pl.BlockSpec((tm, d), lambda i, j: (i, 0), pipeline_mode=pl.Buffered(1)),
                  pl.BlockSpec((1, d), lambda i, j: (0, 0)),
                  pl.BlockSpec((None, tm, dp), lambda i, j: (layer, i, 0)),
                  pl.BlockSpec((tm, tn), lambda i, j: (i, j)),
                  pl.BlockSpec((None, d, tn), lambda i, j: (layer, 0, j)),
                  pl.BlockSpec((None, dp, tn), lambda i, j: (layer, 0, j))],
        out_specs=pl.BlockSpec((tm, tn), lambda i, j: (i, j)),
        out_shape=jax.ShapeDtypeStruct((m, d), F32),
        scratch_shapes=[pltpu.VMEM((tm, d), BF16), pltpu.VMEM((tm, dp), BF16)],
        compiler_params=_cparams(("parallel", "arbitrary")),
        name="ple",
    )(h, g.reshape(1, d), p, h, w_pg, w_pe)


def _rmsnorm_kernel(x_ref, g_ref, o_ref):
    o_ref[...] = _rms(x_ref[...], g_ref[...])


def rmsnorm(x, g, *, tm):
    m, d = x.shape
    return pl.pallas_call(
        _rmsnorm_kernel,
        grid=(m // tm,),
        in_specs=[pl.BlockSpec((tm, d), lambda i: (i, 0)), pl.BlockSpec((1, d), lambda i: (0, 0))],
        out_specs=pl.BlockSpec((tm, d), lambda i: (i, 0)),
        out_shape=jax.ShapeDtypeStruct((m, d), F32),
        compiler_params=_cparams(("parallel",)),
        name="final_norm",
    )(x, g.reshape(1, d))


CONV_HALO = 32
CONV_ROWS = 64


def _ln_swish(y, g, b):
    mu = jnp.mean(y, axis=-1, keepdims=True)
    yc = y - mu
    var = jnp.mean(yc * yc, axis=-1, keepdims=True)
    yn = yc * lax.rsqrt(var + LN_EPS) * g + b
    return yn * _sigmoid(yn)


def _conv_kernel(cur_ref, prev_ref, cw_ref, cb_ref, lg_ref, lb_ref, o_ref, buf_ref, co_ref, *, tt, width):
    t = pl.program_id(1)
    ch = cur_ref.shape[1]
    buf_ref[0:CONV_HALO, :] = jnp.where(t > 0, prev_ref[tt - CONV_HALO:tt, :], 0.0)
    buf_ref[CONV_HALO:, :] = cur_ref[...]
    off = CONV_HALO - (width - 1)
    for lc in range(ch // LANES):
        lanes = slice(lc * LANES, (lc + 1) * LANES)
        for tc in range(tt // CONV_ROWS):
            r0 = off + tc * CONV_ROWS
            acc = jnp.zeros((CONV_ROWS, LANES), F32) + cb_ref[:, lanes]
            for w in range(width):
                acc = acc + buf_ref[r0 + w:r0 + w + CONV_ROWS, lanes] * cw_ref[w:w + 1, lanes]
            co_ref[tc * CONV_ROWS:(tc + 1) * CONV_ROWS, lanes] = acc
    o_ref[...] = _ln_swish(co_ref[...], lg_ref[...], lb_ref[...]).astype(o_ref.dtype)


def conv_branch(glu, conv_w, conv_b, ln_g, ln_b, *, n_seq, tt=256):
    m, ch = glu.shape
    width = conv_w.shape[0]
    nt = m // n_seq // tt
    vec = pl.BlockSpec((1, ch), lambda b, t: (0, 0))
    return pl.pallas_call(
        functools.partial(_conv_kernel, tt=tt, width=width),
        grid=(n_seq, nt),
        in_specs=[pl.BlockSpec((tt, ch), lambda b, t: (b * nt + t, 0)),
                  pl.BlockSpec((tt, ch), lambda b, t: (jnp.maximum(b * nt + t - 1, 0), 0)),
                  pl.BlockSpec((width, ch), lambda b, t: (0, 0)), vec, vec, vec],
        out_specs=pl.BlockSpec((tt, ch), lambda b, t: (b * nt + t, 0)),
        out_shape=jax.ShapeDtypeStruct((m, ch), BF16),
        scratch_shapes=[pltpu.VMEM((tt + CONV_HALO, ch), F32), pltpu.VMEM((tt, ch), F32)],
        compiler_params=_cparams(("parallel", "arbitrary")),
        name="conv_branch",
    )(glu, glu, conv_w, conv_b.reshape(1, ch), ln_g.reshape(1, ch), ln_b.reshape(1, ch))


DEAD_LOG = -104.0


def _log_sigmoid(z):
    return jnp.minimum(z, 0.0) - jnp.log(1.0 + jnp.exp(-jnp.abs(z)))


def _split_bf16(x):
    hi = x.astype(BF16)
    return hi, (x - hi.astype(F32)).astype(BF16)


def _dot_nt(a, b):
    return lax.dot_general(a, b, (((1,), (1,)), ((), ())), preferred_element_type=F32)


SB_BLOCK = 256


def _sb_kernel(q_ref, k_ref, v_ref, o_ref, kb_ref, vt_ref, acc_ref, carry_ref, *, n_blocks):
    blk = SB_BLOCK
    scale = HEAD_DIM ** -0.5
    kb_ref[...] = k_ref[...].astype(BF16)
    vt_ref[...] = v_ref[...].T.astype(BF16)
    row = lax.broadcasted_iota(jnp.int32, (blk, blk), 0)
    col = lax.broadcasted_iota(jnp.int32, (blk, blk), 1)
    after = jnp.where(col > row, 1.0, 0.0).astype(BF16)
    mask = row < col

    def part(q, keys, after, mask, carry):
        z = _dot_nt(kb_ref[keys, :], q) * scale
        log_take = _log_sigmoid(z)
        log_stay = log_take - z
        if mask is not None:
            log_stay = jnp.where(mask, log_stay, 0.0)
        hi, lo = _split_bf16(log_stay)
        later = carry + (jnp.dot(after, hi, preferred_element_type=F32) + jnp.dot(after, lo, preferred_element_type=F32))
        a = jnp.exp(log_take + later)
        if mask is not None:
            a = jnp.where(mask, a, 0.0)
        out_t = jnp.dot(vt_ref[:, keys], a.astype(BF16), preferred_element_type=F32)
        return out_t, carry + jnp.sum(log_stay, axis=0, keepdims=True)

    def own(i):
        return slice(i * blk, (i + 1) * blk)

    zero = jnp.zeros((1, blk), F32)
    for i in range(n_blocks):
        q = q_ref[own(i), :]
        out_t, carry = part(q, own(i), after, mask, zero)
        if i >= 1:
            o, carry = part(q, own(i - 1), after, None, carry)
            out_t = out_t + o
        acc_ref[i] = out_t
        carry_ref[i] = carry

    for i in range(2, n_blocks):
        def body(state, i=i):
            j, _ = state
            keys = pl.ds(pl.multiple_of(j * blk, blk), blk)
            o, c = part(q_ref[own(i), :], keys, after, None, carry_ref[i])
            acc_ref[i] += o
            carry_ref[i] = c
            return j - 1, jnp.max(c) > DEAD_LOG

        lax.while_loop(lambda s: jnp.logical_and(s[0] >= 0, s[1]), body, (i - 2, jnp.max(carry_ref[i]) > DEAD_LOG))

    for i in range(n_blocks):
        o_ref[own(i), :] = acc_ref[i].T.astype(o_ref.dtype)


def sb_attention(q, k, v, *, n_seq, layer):
    m, d_br = q.shape
    s = m // n_seq
    kv = pl.BlockSpec((None, s, HEAD_DIM), lambda b, h: (layer, b, h))
    qo = pl.BlockSpec((s, HEAD_DIM), lambda b, h: (b, h))
    return pl.pallas_call(
        functools.partial(_sb_kernel, n_blocks=s // SB_BLOCK),
        grid=(n_seq, d_br // HEAD_DIM),
        in_specs=[qo, kv, kv],
        out_specs=qo,
        out_shape=jax.ShapeDtypeStruct((m, d_br), BF16),
        scratch_shapes=[pltpu.VMEM((s, HEAD_DIM), BF16), pltpu.VMEM((HEAD_DIM, s), BF16),
                        pltpu.VMEM((s // SB_BLOCK, HEAD_DIM, SB_BLOCK), F32),
                        pltpu.VMEM((s // SB_BLOCK, 1, SB_BLOCK), F32)],
        compiler_params=_cparams(("parallel", "parallel")),
        name="sb_attention",
    )(q, k, v)


MOBA_BLOCK = 256
MOBA_TOPK = 3
NEG_BIG = -1e30


def _moba_kernel(q_ref, k_ref, v_ref, o_ref, vt_ref, s_ref, *, n_blocks):
    blk = MOBA_BLOCK
    nb_pad = -(-n_blocks // 8) * 8
    k = k_ref[...]
    kb = k.astype(BF16)
    vt_ref[...] = v_ref[...].T.astype(BF16)
    means = [jnp.mean(k[n * blk:(n + 1) * blk], axis=0, keepdims=True) for n in range(n_blocks)]
    if nb_pad > n_blocks:
        means.append(jnp.zeros((nb_pad - n_blocks, k.shape[1]), F32))
    km_hi, km_lo = _split_bf16(jnp.concatenate(means, axis=0))
    q = q_ref[...]
    q_hi, q_lo = _split_bf16(q)
    gate = _dot_nt(km_hi, q_hi) + (_dot_nt(km_hi, q_lo) + _dot_nt(km_lo, q_hi))
    qs = (q * (HEAD_DIM ** -0.5)).astype(BF16)

    blk_id = lax.broadcasted_iota(jnp.int32, (nb_pad, blk), 0)
    key = lax.broadcasted_iota(jnp.int32, (blk, blk), 0)
    qry = lax.broadcasted_iota(jnp.int32, (blk, blk), 1)
    for i in range(n_blocks):
        own = slice(i * blk, (i + 1) * blk)
        n = (i + 1) * blk
        st = _dot_nt(kb[:n], qs[own])
        g = gate[:, own]
        for j in range(i):
            gj = g[j:j + 1, :]
            ahead = jnp.logical_and(blk_id < i, jnp.logical_or(g > gj, jnp.logical_and(g == gj, blk_id < j)))
            rank = jnp.sum(jnp.where(ahead, 1.0, 0.0), axis=0, keepdims=True)
            rows = slice(j * blk, (j + 1) * blk)
            s_ref[rows, :] = jnp.where(rank < MOBA_TOPK, st[rows], NEG_BIG)
        s_ref[own, :] = jnp.where(key <= qry, st[own], NEG_BIG)
        s = s_ref[:n, :]
        m = jnp.max(s, axis=0, keepdims=True)
        p = jnp.exp(s - m)
        l = jnp.sum(p, axis=0, keepdims=True)
        ot = jnp.dot(vt_ref[:, :n], p.astype(BF16), preferred_element_type=F32)
        o_ref[own, :] = (ot / l).T.astype(o_ref.dtype)


def moba_attention(q, k, v, *, n_seq, layer):
    m, d_br = q.shape
    s = m // n_seq
    kv = pl.BlockSpec((None, s, HEAD_DIM), lambda b, h: (layer, b, h))
    qo = pl.BlockSpec((s, HEAD_DIM), lambda b, h: (b, h))
    return pl.pallas_call(
        functools.partial(_moba_kernel, n_blocks=s // MOBA_BLOCK),
        grid=(n_seq, d_br // HEAD_DIM),
        in_specs=[qo, kv, kv],
        out_specs=qo,
        out_shape=jax.ShapeDtypeStruct((m, d_br), BF16),
        scratch_shapes=[pltpu.VMEM((HEAD_DIM, s), BF16), pltpu.VMEM((s, MOBA_BLOCK), F32)],
        compiler_params=_cparams(("parallel", "parallel")),
        name="moba_attention",
    )(q, k, v)


def _sb_step_kernel(pt_ref, q_ref, kc_ref, vc_ref, o_ref, kbuf, vbuf, sem, acc_ref, carry_ref, *, layer, n_pages):
    b = pl.program_id(0)
    page_len, n_heads, d = kbuf.shape[1:]
    flat = page_len * n_heads
    n_chunks = flat // LANES
    scale = HEAD_DIM ** -0.5

    def copies(p, slot):
        page = pt_ref[b * n_pages + p]
        return (pltpu.make_async_copy(kc_ref.at[layer, page], kbuf.at[slot], sem.at[0, slot]),
                pltpu.make_async_copy(vc_ref.at[layer, page], vbuf.at[slot], sem.at[1, slot]))

    def start(p, slot):
        for c in copies(p, slot):
            c.start()

    def wait(p, slot):
        for c in copies(p, slot):
            c.wait()

    q = q_ref[...].astype(BF16)
    sub = lax.broadcasted_iota(jnp.int32, (n_heads, flat), 0)
    lane = lax.broadcasted_iota(jnp.int32, (n_heads, flat), 1)
    own_head = (lane % n_heads) == sub
    r2 = lax.broadcasted_iota(jnp.int32, (LANES, LANES), 0)
    c2 = lax.broadcasted_iota(jnp.int32, (LANES, LANES), 1)
    after = jnp.where(r2 > c2, 1.0, 0.0).astype(BF16)
    acc_ref[...] = jnp.zeros_like(acc_ref)
    carry_ref[...] = jnp.zeros_like(carry_ref)
    start(n_pages - 1, 0)

    def body(state):
        p, _ = state
        slot = (n_pages - 1 - p) % 2

        @pl.when(p > 0)
        def _():
            start(p - 1, 1 - slot)

        wait(p, slot)
        kf = kbuf[slot].reshape(flat, d).astype(BF16)
        vf = vbuf[slot].reshape(flat, d).astype(BF16)
        z = _dot_nt(q, kf) * scale
        log_take = _log_sigmoid(z)
        log_stay = jnp.where(own_head, log_take - z, 0.0)
        stacked = jnp.concatenate([log_stay[:, c * LANES:(c + 1) * LANES] for c in range(n_chunks)], axis=0)
        hi, lo = _split_bf16(stacked)
        within = jnp.dot(hi, after, preferred_element_type=F32) + jnp.dot(lo, after, preferred_element_type=F32)
        total = jnp.sum(stacked, axis=1, keepdims=True)
        run = carry_ref[...]
        pieces = [None] * n_chunks
        for c in range(n_chunks - 1, -1, -1):
            rows = slice(c * n_heads, (c + 1) * n_heads)
            lanes = slice(c * LANES, (c + 1) * LANES)
            later = run + within[rows, :]
            pieces[c] = jnp.where(own_head[:, lanes], jnp.exp(log_take[:, lanes] + later), 0.0)
            run = run + total[rows, :]
        a = jnp.concatenate(pieces, axis=1).astype(BF16)
        acc_ref[...] += jnp.dot(a, vf, preferred_element_type=F32)
        carry_ref[...] = run
        return p - 1, jnp.max(run) > DEAD_LOG

    p_end, _ = lax.while_loop(lambda s: jnp.logical_and(s[0] >= 0, s[1]), body, (n_pages - 1, True))

    @pl.when(p_end >= 0)
    def _():
        wait(p_end, (n_pages - 1 - p_end) % 2)

    o_ref[...] = acc_ref[...]


def sb_step(q, k_cache, v_cache, page_table, *, layer):
    n_seq, n_heads, d = q.shape
    n_pages = page_table.shape[1]
    page_len = k_cache.shape[2]
    tok = pl.BlockSpec((None, n_heads, d), lambda b, pt: (b, 0, 0))
    return pl.pallas_call(
        functools.partial(_sb_step_kernel, layer=layer, n_pages=n_pages),
        grid_spec=pltpu.PrefetchScalarGridSpec(
            num_scalar_prefetch=1,
            grid=(n_seq,),
            in_specs=[tok, pl.BlockSpec(memory_space=pl.ANY), pl.BlockSpec(memory_space=pl.ANY)],
            out_specs=tok,
            scratch_shapes=[pltpu.VMEM((2, page_len, n_heads, d), F32), pltpu.VMEM((2, page_len, n_heads, d), F32),
                            pltpu.SemaphoreType.DMA((2, 2)),
                            pltpu.VMEM((n_heads, d), F32), pltpu.VMEM((n_heads, 1), F32)]),
        out_shape=jax.ShapeDtypeStruct((n_seq, n_heads, d), F32),
        compiler_params=_cparams(("arbitrary",)),
        name="sb_step",
    )(page_table.reshape(-1), q, k_cache, v_cache)


GATE_PAGES = 8


def _moba_gate_kernel(pt_ref, q_ref, *refs, pages_per_block, n_blocks):
    pages = refs[:GATE_PAGES]
    idx_ref, g_ref = refs[GATE_PAGES:]
    s = pl.program_id(1)
    per_step = GATE_PAGES // pages_per_block
    block_len = pages_per_block * pages[0].shape[0]

    @pl.when(s == 0)
    def _():
        g_ref[...] = jnp.full_like(g_ref, -jnp.inf)

    q = q_ref[...]
    lane = lax.broadcasted_iota(jnp.int32, g_ref.shape, 1)
    g = g_ref[...]
    for n in range(per_step):
        ksum = jnp.sum(pages[n * pages_per_block][...], axis=0)
        for r in range(1, pages_per_block):
            ksum = ksum + jnp.sum(pages[n * pages_per_block + r][...], axis=0)
        gn = jnp.sum(ksum * q, axis=1, keepdims=True) * (1.0 / block_len)
        g = jnp.where(lane == s * per_step + n, gn, g)
    g_ref[...] = g

    @pl.when(s == pl.num_programs(1) - 1)
    def _():
        lane_f = lane.astype(F32)
        gg = g
        out = jnp.zeros(g_ref.shape, jnp.int32)
        for r in range(MOBA_TOPK):
            best = jnp.max(gg, axis=1, keepdims=True)
            first = jnp.min(jnp.where(gg == best, lane_f, float(LANES)), axis=1, keepdims=True)
            out = jnp.where(lane == r, first.astype(jnp.int32), out)
            gg = jnp.where(lane_f == first, -jnp.inf, gg)
        idx_ref[...] = out


def moba_gate(q, k_cache, page_table, *, layer):
    n_seq, n_heads, d = q.shape
    n_pages = page_table.shape[1]
    page_len = k_cache.shape[2]
    pages_per_block = MOBA_BLOCK // page_len
    n_blocks = n_pages // pages_per_block
    assert MOBA_TOPK <= n_blocks <= LANES and n_pages % GATE_PAGES == 0 and GATE_PAGES % pages_per_block == 0
    tok = pl.BlockSpec((None, n_heads, d), lambda b, s, pt: (b, 0, 0))
    page_specs = [
        pl.BlockSpec((None, None, page_len, n_heads, d),
                     functools.partial(lambda b, s, pt, r: (layer, pt[b * n_pages + s * GATE_PAGES + r], 0, 0, 0), r=r))
        for r in range(GATE_PAGES)]
    return pl.pallas_call(
        functools.partial(_moba_gate_kernel, pages_per_block=pages_per_block, n_blocks=n_blocks),
        grid_spec=pltpu.PrefetchScalarGridSpec(
            num_scalar_prefetch=1,
            grid=(n_seq, n_pages // GATE_PAGES),
            in_specs=[tok] + page_specs,
            out_specs=pl.BlockSpec((None, n_heads, LANES), lambda b, s, pt: (b, 0, 0)),
            scratch_shapes=[pltpu.VMEM((n_heads, LANES), F32)]),
        out_shape=jax.ShapeDtypeStruct((n_seq, n_heads, LANES), jnp.int32),
        compiler_params=_cparams(("parallel", "arbitrary")),
        name="moba_gate",
    )(page_table.reshape(-1), q, *([k_cache] * GATE_PAGES))


def _moba_step_kernel(pt_ref, sel_ref, q_ref, kn_ref, vn_ref, kc_ref, vc_ref, o_ref, kg, vg, sem, *,
                      layer, n_pages, pages_per_block):
    b = pl.program_id(0)
    n_heads, d = q_ref.shape
    page_len = kg.shape[1] // (MOBA_TOPK * pages_per_block)
    scale = HEAD_DIM ** -0.5

    def copies():
        out = []
        for h in range(n_heads):
            for r in range(MOBA_TOPK):
                blk = sel_ref[(b * n_heads + h) * MOBA_TOPK + r]
                for g in range(pages_per_block):
                    page = pt_ref[b * n_pages + blk * pages_per_block + g]
                    rows = pl.ds((r * pages_per_block + g) * page_len, page_len)
                    out.append(pltpu.make_async_copy(kc_ref.at[layer, page, :, h, :], kg.at[h, rows, :], sem.at[0]))
                    out.append(pltpu.make_async_copy(vc_ref.at[layer, page, :, h, :], vg.at[h, rows, :], sem.at[1]))
        return out

    cps = copies()
    for c in cps:
        c.start()
    for c in cps:
        c.wait()

    qf = q_ref[...]
    q = qf.astype(BF16)
    sub = lax.broadcasted_iota(jnp.int32, (n_heads, d), 0)
    s_own = jnp.sum(qf * kn_ref[...], axis=1, keepdims=True) * scale
    res = jnp.zeros((n_heads, d), F32)
    for h in range(n_heads):
        s = _dot_nt(q, kg[h].astype(BF16)) * scale
        m = jnp.maximum(jnp.max(s, axis=1, keepdims=True), s_own)
        p = jnp.exp(s - m)
        p_own = jnp.exp(s_own - m)
        l = jnp.sum(p, axis=1, keepdims=True) + p_own
        o = (jnp.dot(p.astype(BF16), vg[h].astype(BF16), preferred_element_type=F32) + p_own * vn_ref[...]) / l
        res = jnp.where(sub == h, o, res)
    o_ref[...] = res


def moba_step(q, k_new, v_new, sel, k_cache, v_cache, page_table, *, layer):
    n_seq, n_heads, d = q.shape
    n_pages = page_table.shape[1]
    page_len = k_cache.shape[2]
    pages_per_block = MOBA_BLOCK // page_len
    tok = pl.BlockSpec((None, n_heads, d), lambda b, pt, sl: (b, 0, 0))
    gathered = pltpu.VMEM((n_heads, MOBA_TOPK * MOBA_BLOCK, d), F32)
    return pl.pallas_call(
        functools.partial(_moba_step_kernel, layer=layer, n_pages=n_pages, pages_per_block=pages_per_block),
        grid_spec=pltpu.PrefetchScalarGridSpec(
            num_scalar_prefetch=2,
            grid=(n_seq,),
            in_specs=[tok, tok, tok, pl.BlockSpec(memory_space=pl.ANY), pl.BlockSpec(memory_space=pl.ANY)],
            out_specs=tok,
            scratch_shapes=[gathered, gathered, pltpu.SemaphoreType.DMA((2,))]),
        out_shape=jax.ShapeDtypeStruct((n_seq, n_heads, d), F32),
        compiler_params=_cparams(("arbitrary",)),
        name="moba_step",
    )(page_table.reshape(-1), sel[:, :, :MOBA_TOPK].reshape(-1), q, k_new, v_new, k_cache, v_cache)


def _conv_step_kernel(st_ref, glu_ref, cw_ref, cb_ref, lg_ref, lb_ref, y_ref, ns_ref, *, width):
    n_seq = st_ref.shape[0]
    past = width - 1
    rows = []
    for b in range(n_seq):
        st = st_ref[b]
        new = glu_ref[b:b + 1, :]
        co = jnp.sum(st * cw_ref[0:past, :], axis=0, keepdims=True) + new * cw_ref[past:width, :] + cb_ref[...]
        rows.append(_ln_swish(co, lg_ref[...], lb_ref[...]))
        ns_ref[b, 0:past - 1, :] = st[1:past]
        ns_ref[b, past - 1:past, :] = new
    y_ref[...] = jnp.concatenate(rows, axis=0).astype(y_ref.dtype)


def conv_step(state, glu, conv_w, conv_b, ln_g, ln_b):
    n_seq, past, ch = state.shape
    width = conv_w.shape[0]
    return pl.pallas_call(
        functools.partial(_conv_step_kernel, width=width),
        out_shape=[jax.ShapeDtypeStruct((n_seq, ch), BF16), jax.ShapeDtypeStruct((n_seq, past, ch), F32)],
        compiler_params=pltpu.CompilerParams(vmem_limit_bytes=VMEM_LIMIT),
        name="conv_step",
    )(state, glu, conv_w, conv_b.reshape(1, ch), ln_g.reshape(1, ch), ln_b.reshape(1, ch))


PROMPT_ROWS = 1024


def kernel(x_prompt, x_sample, p_prompt, p_sample, cache_sb_k, cache_sb_v, cache_mb_k, cache_mb_v, state_conv,
           page_table, norm_ffn1, w_ff1_in, w_ff1_out, norm_mix, w_in, conv_w, conv_b, conv_ln_g, conv_ln_b, w_br,
           w_gate, b_gate, w_o, norm_ffn2, w_ff2_in, w_ff2_out, norm_ple, w_pg, w_pe, norm_final):
    depth = w_in.shape[0]
    n_seq, seq, d = x_prompt.shape
    n_dec, dec_seq, _ = x_sample.shape
    past = state_conv.shape[2]
    d_br = w_br.shape[2]
    n_heads = d_br // HEAD_DIM
    assert dec_seq == 1 and seq >= past and seq % MOBA_BLOCK == 0
    m = n_seq * seq
    tm = min(PROMPT_ROWS, m)

    def trunk_layer(layer, h, p, rows, kv_bufs, branches):
        h = ffn(h, norm_ffn1[layer], w_ff1_in, w_ff1_out, layer=layer, tm=rows)
        glu, q_sb, k_sb, v_sb, q_mb, k_mb, v_mb = inproj(h, norm_mix[layer], w_in, layer=layer, tm=rows, kv_bufs=kv_bufs)
        kv_bufs = (k_sb, v_sb, k_mb, v_mb)
        y_a, y_b, y_c, aux = branches(glu, q_sb, q_mb, kv_bufs)
        mixed = mix(h, norm_mix[layer], y_a, y_b, y_c, w_gate, b_gate[layer], w_br, layer=layer, tm=rows)
        h = matres(mixed, w_o, h, layer=layer, tm=rows)
        h = ffn(h, norm_ffn2[layer], w_ff2_in, w_ff2_out, layer=layer, tm=rows)
        h = ple(h, norm_ple[layer], p, w_pg, w_pe, layer=layer, tm=rows)
        return h, kv_bufs, aux

    hp = x_prompt.reshape(m, d)
    hs = x_sample.reshape(n_dec, d)
    pp = p_prompt.reshape(depth, m, -1)
    ps = p_sample.reshape(depth, n_dec, -1)
    kv_p = (jnp.zeros((depth, m, d_br), F32),) * 4
    kv_s = (jnp.zeros((depth, n_dec, d_br), F32),) * 4
    conv_p, conv_s = [], []
    for layer in range(depth):
        conv_args = (conv_w[layer], conv_b[layer], conv_ln_g[layer], conv_ln_b[layer])

        def prompt_branches(glu, q_sb, q_mb, kv, layer=layer, conv_args=conv_args):
            y_a = conv_branch(glu, *conv_args, n_seq=n_seq)
            y_b = sb_attention(q_sb, kv[0], kv[1], n_seq=n_seq, layer=layer)
            y_c = moba_attention(q_mb, kv[2], kv[3], n_seq=n_seq, layer=layer)
            return y_a, y_b, y_c, glu.reshape(n_seq, seq, d_br)[:, seq - past:]

        def sample_branches(glu, q_sb, q_mb, kv, layer=layer, conv_args=conv_args):
            y_a, new_state = conv_step(state_conv[layer], glu, *conv_args)
            tok = (n_dec, n_heads, HEAD_DIM)
            y_b = sb_step(q_sb.astype(F32).reshape(tok), cache_sb_k, cache_sb_v, page_table, layer=layer)
            q3 = q_mb.reshape(tok)
            sel = moba_gate(q3, cache_mb_k, page_table, layer=layer)
            y_c = moba_step(q3, kv[2][layer].reshape(tok), kv[3][layer].reshape(tok), sel, cache_mb_k, cache_mb_v,
                            page_table, layer=layer)
            return y_a, y_b.reshape(n_dec, d_br).astype(BF16), y_c.reshape(n_dec, d_br).astype(BF16), new_state

        hp, kv_p, aux_p = trunk_layer(layer, hp, pp, tm, kv_p, prompt_branches)
        hs, kv_s, aux_s = trunk_layer(layer, hs, ps, n_dec, kv_s, sample_branches)
        conv_p.append(aux_p)
        conv_s.append(aux_s)

    y_prompt = rmsnorm(hp, norm_final, tm=tm).reshape(n_seq, seq, d)
    y_sample = rmsnorm(hs, norm_final, tm=n_dec).reshape(n_dec, 1, d)
    rows_p = [a.reshape(depth, n_seq, seq, n_heads, HEAD_DIM) for a in kv_p]
    rows_s = [a.reshape(depth, n_dec, 1, n_heads, HEAD_DIM) for a in kv_s]
    return (y_prompt, y_sample, *rows_p, jnp.stack(conv_p), *rows_s, jnp.stack(conv_s))
```

```python
import functools

import jax
import jax.numpy as jnp
from jax import lax
from jax.experimental import pallas as pl
from jax.experimental.pallas import tpu as pltpu

F32 = jnp.float32
BF16 = jnp.bfloat16

RMS_EPS = 1e-6
LN_EPS = 1e-5
HEAD_DIM = 128
LANES = 128
VMEM_LIMIT = 56 * 1024 * 1024


def _cparams(sem):
    return pltpu.CompilerParams(dimension_semantics=sem, vmem_limit_bytes=VMEM_LIMIT)


def _rms(x, g):
    ms = jnp.mean(x * x, axis=-1, keepdims=True)
    return x * lax.rsqrt(ms + RMS_EPS) * g


def _sigmoid(x):
    return 1.0 / (1.0 + jnp.exp(-x))


FFN_CHUNKS = 2


def _ffn_kernel(*refs, n_chunks):
    nc = FFN_CHUNKS
    x_ref, g_ref = refs[:2]
    wa, wb, wd = refs[2:2 + nc], refs[2 + nc:2 + 2 * nc], refs[2 + 2 * nc:2 + 3 * nc]
    o_ref, xn_ref, wup_ref, wdn_ref = refs[2 + 3 * nc:]
    f = pl.program_id(1)
    c = LANES

    @pl.when(f == 0)
    def _():
        xn_ref[...] = _rms(x_ref[...], g_ref[...]).astype(BF16)
        o_ref[...] = jnp.zeros_like(o_ref)

    for k in range(nc):
        wup_ref[:, k * c:(k + 1) * c] = wa[k][...].astype(BF16)
        wup_ref[:, (nc + k) * c:(nc + k + 1) * c] = wb[k][...].astype(BF16)
        wdk = wd[k][...] if k == 0 else jnp.where(nc * f + k < n_chunks, wd[k][...], 0.0)
        wdn_ref[k * c:(k + 1) * c, :] = wdk.astype(BF16)

    ab = jnp.dot(xn_ref[...], wup_ref[...], preferred_element_type=F32)
    a = ab[:, :nc * c]
    b = ab[:, nc * c:]
    act = (a * _sigmoid(a) * b).astype(BF16)
    o_ref[...] += jnp.dot(act, wdn_ref[...], preferred_element_type=F32)

    @pl.when(f == pl.num_programs(1) - 1)
    def _():
        o_ref[...] = x_ref[...] + 0.5 * o_ref[...]


def ffn(x, g, w_up, w_down, *, layer, tm):
    m, d = x.shape
    d_ff = w_down.shape[1]
    c = LANES
    nc = FFN_CHUNKS
    n_chunks = d_ff // c
    n_steps = -(-n_chunks // nc)
    last = n_chunks - 1

    def up(off, k):
        return pl.BlockSpec((None, d, c), lambda i, f: (layer, 0, off + jnp.minimum(nc * f + k, last)))

    def down(k):
        return pl.BlockSpec((None, c, d), lambda i, f: (layer, jnp.minimum(nc * f + k, last), 0))

    row = pl.BlockSpec((tm, d), lambda i, f: (i, 0))
    return pl.pallas_call(
        functools.partial(_ffn_kernel, n_chunks=n_chunks),
        grid=(m // tm, n_steps),
        in_specs=[row, pl.BlockSpec((1, d), lambda i, f: (0, 0))]
        + [up(0, k) for k in range(nc)] + [up(n_chunks, k) for k in range(nc)] + [down(k) for k in range(nc)],
        out_specs=pl.BlockSpec((tm, d), lambda i, f: (i, 0), pipeline_mode=pl.Buffered(1)),
        out_shape=jax.ShapeDtypeStruct((m, d), F32),
        scratch_shapes=[pltpu.VMEM((tm, d), BF16), pltpu.VMEM((d, 2 * nc * c), BF16), pltpu.VMEM((nc * c, d), BF16)],
        compiler_params=_cparams(("parallel", "arbitrary")),
        name="ffn",
    )(x, g.reshape(1, d), *([w_up] * (2 * nc)), *([w_down] * nc))


N_IN = 8


def _inproj_kernel(*refs, tn, n_alias):
    h_ref, g_ref = refs[:2]
    w_refs = refs[2:2 + N_IN]
    outs = refs[2 + N_IN + n_alias:2 + N_IN + n_alias + 7]
    u_ref, wcat_ref = refs[-2:]

    @pl.when(pl.program_id(1) == 0)
    def _():
        u_ref[...] = _rms(h_ref[...], g_ref[...]).astype(BF16)

    for c in range(N_IN):
        wcat_ref[:, c * tn:(c + 1) * tn] = w_refs[c][...].astype(BF16)
    y = jnp.dot(u_ref[...], wcat_ref[...], preferred_element_type=F32)
    outs[0][...] = y[:, :tn] * _sigmoid(y[:, tn:2 * tn])
    for c in range(2, N_IN):
        outs[c - 1][...] = y[:, c * tn:(c + 1) * tn].astype(outs[c - 1].dtype)


def inproj(h, g, w_in, *, layer, tm, tn=128, kv_bufs=None):
    m, d = h.shape
    depth = w_in.shape[0]
    d_br = w_in.shape[2] // N_IN
    nj = d_br // tn
    row = pl.BlockSpec((tm, d), lambda i, j: (i, 0))
    w_specs = [pl.BlockSpec((None, d, tn), functools.partial(lambda i, j, c: (layer, 0, c * nj + j), c=c))
               for c in range(N_IN)]
    flat = pl.BlockSpec((tm, tn), lambda i, j: (i, j))
    stacked = pl.BlockSpec((None, tm, tn), lambda i, j: (layer, i, j))
    flat_f32 = jax.ShapeDtypeStruct((m, d_br), F32)
    stacked_f32 = jax.ShapeDtypeStruct((depth, m, d_br), F32)
    out_specs = [flat, flat, stacked, stacked, flat, stacked, stacked]
    out_shape = [flat_f32, jax.ShapeDtypeStruct((m, d_br), BF16), stacked_f32, stacked_f32,
                 flat_f32, stacked_f32, stacked_f32]
    alias_in, aliases = [], {}
    if kv_bufs is not None:
        alias_in = list(kv_bufs)
        for n, o in enumerate((2, 3, 5, 6)):
            aliases[2 + N_IN + n] = o
    return pl.pallas_call(
        functools.partial(_inproj_kernel, tn=tn, n_alias=len(alias_in)),
        grid=(m // tm, nj),
        in_specs=[row, pl.BlockSpec((1, d), lambda i, j: (0, 0))] + w_specs
        + [pl.BlockSpec(memory_space=pl.ANY)] * len(alias_in),
        out_specs=out_specs,
        out_shape=out_shape,
        input_output_aliases=aliases,
        scratch_shapes=[pltpu.VMEM((tm, d), BF16), pltpu.VMEM((d, N_IN * tn), BF16)],
        compiler_params=_cparams(("parallel", "arbitrary")),
        name="inproj",
    )(h, g.reshape(1, d), *([w_in] * N_IN), *alias_in)


def _mix_kernel(h_ref, g_ref, ya, yb, yc, wg0, wg1, wg2, bg0, bg1, bg2, wbr, o_ref, u_ref):
    @pl.when(pl.program_id(1) == 0)
    def _():
        u_ref[...] = _rms(h_ref[...], g_ref[...]).astype(BF16)

    u = u_ref[...]
    acc = None
    for b, (y, wg, bg) in enumerate(((ya, wg0, bg0), (yb, wg1, bg1), (yc, wg2, bg2))):
        gate = _sigmoid(jnp.dot(u, wg[...].astype(BF16), preferred_element_type=F32) + bg[...])
        t = gate * jnp.dot(y[...], wbr[b].astype(BF16), preferred_element_type=F32)
        acc = t if acc is None else acc + t
    o_ref[...] = acc.astype(o_ref.dtype)


def mix(h, g, ya, yb, yc, w_gate, b_gate, w_br, *, layer, tm, tn=256):
    m, d = h.shape
    _, nb, d_br, _ = w_br.shape
    nj = d // tn
    row = pl.BlockSpec((tm, d), lambda i, j: (i, 0), pipeline_mode=pl.Buffered(1))
    br = pl.BlockSpec((tm, d_br), lambda i, j: (i, 0), pipeline_mode=pl.Buffered(1))
    wg = [pl.BlockSpec((None, d, tn), functools.partial(lambda i, j, b: (layer, 0, b * nj + j), b=b)) for b in range(nb)]
    bg = [pl.BlockSpec((1, tn), functools.partial(lambda i, j, b: (0, b * nj + j), b=b)) for b in range(nb)]
    b2 = b_gate.reshape(1, nb * d)
    return pl.pallas_call(
        _mix_kernel,
        grid=(m // tm, nj),
        in_specs=[row, pl.BlockSpec((1, d), lambda i, j: (0, 0)), br, br, br] + wg + bg
        + [pl.BlockSpec((None, nb, d_br, tn), lambda i, j: (layer, 0, 0, j))],
        out_specs=pl.BlockSpec((tm, tn), lambda i, j: (i, j)),
        out_shape=jax.ShapeDtypeStruct((m, d), BF16),
        scratch_shapes=[pltpu.VMEM((tm, d), BF16)],
        compiler_params=_cparams(("parallel", "arbitrary")),
        name="mix",
    )(h, g.reshape(1, d), ya, yb, yc, w_gate, w_gate, w_gate, b2, b2, b2, w_br)


def _matres_kernel(a_ref, w_ref, r_ref, o_ref):
    o_ref[...] = r_ref[...] + jnp.dot(a_ref[...], w_ref[...].astype(BF16), preferred_element_type=F32)


def matres(a, w, res, *, layer, tm, tn=256):
    m, k = a.shape
    n = w.shape[2]
    return pl.pallas_call(
        _matres_kernel,
        grid=(m // tm, n // tn),
        in_specs=[pl.BlockSpec((tm, k), lambda i, j: (i, 0)), pl.BlockSpec((None, k, tn), lambda i, j: (layer, 0, j)),
                  pl.BlockSpec((tm, tn), lambda i, j: (i, j))],
        out_specs=pl.BlockSpec((tm, tn), lambda i, j: (i, j)),
        out_shape=jax.ShapeDtypeStruct((m, n), F32),
        compiler_params=_cparams(("parallel", "arbitrary")),
        name="matres",
    )(a, w, res)


def _ple_kernel(h_ref, g_ref, p_ref, hj_ref, wpg, wpe, o_ref, hn_ref, pb_ref):
    @pl.when(pl.program_id(1) == 0)
    def _():
        hn_ref[...] = _rms(h_ref[...], g_ref[...]).astype(BF16)
        pb_ref[...] = p_ref[...].astype(BF16)

    gate = _sigmoid(jnp.dot(hn_ref[...], wpg[...].astype(BF16), preferred_element_type=F32))
    e = jnp.dot(pb_ref[...], wpe[...].astype(BF16), preferred_element_type=F32)
    o_ref[...] = hj_ref[...] + gate * e


def ple(h, g, p, w_pg, w_pe, *, layer, tm, tn=256):
    m, d = h.shape
    dp = p.shape[2]
    return pl.pallas_call(
        _ple_kernel,
        grid=(m // tm, d // tn),
        in_specs=[pl.BlockSpec((tm, d), lambda i, j: (i, 0)),
                  pl.BlockSpec((1, d), lambda i, j: (0, 0)),
                  pl.BlockSpec((None, tm, dp), lambda i, j: (layer, i, 0)),
                  pl.BlockSpec((tm, tn), lambda i, j: (i, j)),
                  pl.BlockSpec((None, d, tn), lambda i, j: (layer, 0, j)),
                  pl.BlockSpec((None, dp, tn), lambda i, j: (layer, 0, j))],
        out_specs=pl.BlockSpec((tm, tn), lambda i, j: (i, j)),
        out_shape=jax.ShapeDtypeStruct((m, d), F32),
        scratch_shapes=[pltpu.VMEM((tm, d), BF16), pltpu.VMEM((tm, dp), BF16)],
        compiler_params=_cparams(("parallel", "arbitrary")),
        name="ple",
    )(h, g.reshape(1, d), p, h, w_pg, w_pe)


def _rmsnorm_kernel(x_ref, g_ref, o_ref):
    o_ref[...] = _rms(x_ref[...], g_ref[...])


def rmsnorm(x, g, *, tm):
    m, d = x.shape
    return pl.pallas_call(
        _rmsnorm_kernel,
        grid=(m // tm,),
        in_specs=[pl.BlockSpec((tm, d), lambda i: (i, 0)), pl.BlockSpec((1, d), lambda i: (0, 0))],
        out_specs=pl.BlockSpec((tm, d), lambda i: (i, 0)),
        out_shape=jax.ShapeDtypeStruct((m, d), F32),
        compiler_params=_cparams(("parallel",)),
        name="final_norm",
    )(x, g.reshape(1, d))


CONV_HALO = 32
CONV_ROWS = 64


def _ln_swish(y, g, b):
    mu = jnp.mean(y, axis=-1, keepdims=True)
    yc = y - mu
    var = jnp.mean(yc * yc, axis=-1, keepdims=True)
    yn = yc * lax.rsqrt(var + LN_EPS) * g + b
    return yn * _sigmoid(yn)


def _conv_kernel(cur_ref, prev_ref, cw_ref, cb_ref, lg_ref, lb_ref, o_ref, buf_ref, co_ref, *, tt, width):
    t = pl.program_id(1)
    ch = cur_ref.shape[1]
    buf_ref[0:CONV_HALO, :] = jnp.where(t > 0, prev_ref[tt - CONV_HALO:tt, :], 0.0)
    buf_ref[CONV_HALO:, :] = cur_ref[...]
    off = CONV_HALO - (width - 1)
    for lc in range(ch // LANES):
        lanes = slice(lc * LANES, (lc + 1) * LANES)
        for tc in range(tt // CONV_ROWS):
            r0 = off + tc * CONV_ROWS
            acc = jnp.zeros((CONV_ROWS, LANES), F32) + cb_ref[:, lanes]
            for w in range(width):
                acc = acc + buf_ref[r0 + w:r0 + w + CONV_ROWS, lanes] * cw_ref[w:w + 1, lanes]
            co_ref[tc * CONV_ROWS:(tc + 1) * CONV_ROWS, lanes] = acc
    o_ref[...] = _ln_swish(co_ref[...], lg_ref[...], lb_ref[...]).astype(o_ref.dtype)


def conv_branch(glu, conv_w, conv_b, ln_g, ln_b, *, n_seq, tt=256):
    m, ch = glu.shape
    width = conv_w.shape[0]
    nt = m // n_seq // tt
    vec = pl.BlockSpec((1, ch), lambda b, t: (0, 0))
    return pl.pallas_call(
        functools.partial(_conv_kernel, tt=tt, width=width),
        grid=(n_seq, nt),
        in_specs=[pl.BlockSpec((tt, ch), lambda b, t: (b * nt + t, 0)),
                  pl.BlockSpec((tt, ch), lambda b, t: (jnp.maximum(b * nt + t - 1, 0), 0)),
                  pl.BlockSpec((width, ch), lambda b, t: (0, 0)), vec, vec, vec],
        out_specs=pl.BlockSpec((tt, ch), lambda b, t: (b * nt + t, 0)),
        out_shape=jax.ShapeDtypeStruct((m, ch), BF16),
        scratch_shapes=[pltpu.VMEM((tt + CONV_HALO, ch), F32), pltpu.VMEM((tt, ch), F32)],
        compiler_params=_cparams(("parallel", "arbitrary")),
        name="conv_branch",
    )(glu, glu, conv_w, conv_b.reshape(1, ch), ln_g.reshape(1, ch), ln_b.reshape(1, ch))


DEAD_LOG = -104.0


def _log_sigmoid(z):
    return jnp.minimum(z, 0.0) - jnp.log(1.0 + jnp.exp(-jnp.abs(z)))


def _split_bf16(x):
    hi = x.astype(BF16)
    return hi, (x - hi.astype(F32)).astype(BF16)


def _dot_nt(a, b):
    return lax.dot_general(a, b, (((1,), (1,)), ((), ())), preferred_element_type=F32)


SB_BLOCK = 256


def _sb_kernel(q_ref, k_ref, v_ref, o_ref, kb_ref, vt_ref, acc_ref, carry_ref, *, n_blocks):
    blk = SB_BLOCK
    scale = HEAD_DIM ** -0.5
    kb_ref[...] = k_ref[...].astype(BF16)
    vt_ref[...] = v_ref[...].T.astype(BF16)
    row = lax.broadcasted_iota(jnp.int32, (blk, blk), 0)
    col = lax.broadcasted_iota(jnp.int32, (blk, blk), 1)
    after = jnp.where(col > row, 1.0, 0.0).astype(BF16)
    mask = row < col

    def part(q, keys, after, mask, carry):
        z = _dot_nt(kb_ref[keys, :], q) * scale
        log_take = _log_sigmoid(z)
        log_stay = log_take - z
        if mask is not None:
            log_stay = jnp.where(mask, log_stay, 0.0)
        hi, lo = _split_bf16(log_stay)
        later = carry + (jnp.dot(after, hi, preferred_element_type=F32) + jnp.dot(after, lo, preferred_element_type=F32))
        a = jnp.exp(log_take + later)
        if mask is not None:
            a = jnp.where(mask, a, 0.0)
        out_t = jnp.dot(vt_ref[:, keys], a.astype(BF16), preferred_element_type=F32)
        return out_t, carry + jnp.sum(log_stay, axis=0, keepdims=True)

    def own(i):
        return slice(i * blk, (i + 1) * blk)

    zero = jnp.zeros((1, blk), F32)
    for i in range(n_blocks):
        q = q_ref[own(i), :]
        out_t, carry = part(q, own(i), after, mask, zero)
        if i >= 1:
            o, carry = part(q, own(i - 1), after, None, carry)
            out_t = out_t + o
        acc_ref[i] = out_t
        carry_ref[i] = carry

    for i in range(2, n_blocks):
        def body(state, i=i):
            j, _ = state
            keys = pl.ds(pl.multiple_of(j * blk, blk), blk)
            o, c = part(q_ref[own(i), :], keys, after, None, carry_ref[i])
            acc_ref[i] += o
            carry_ref[i] = c
            return j - 1, jnp.max(c) > DEAD_LOG

        lax.while_loop(lambda s: jnp.logical_and(s[0] >= 0, s[1]), body, (i - 2, jnp.max(carry_ref[i]) > DEAD_LOG))

    for i in range(n_blocks):
        o_ref[own(i), :] = acc_ref[i].T.astype(o_ref.dtype)


def sb_attention(q, k, v, *, n_seq, layer):
    m, d_br = q.shape
    s = m // n_seq
    kv = pl.BlockSpec((None, s, HEAD_DIM), lambda b, h: (layer, b, h))
    qo = pl.BlockSpec((s, HEAD_DIM), lambda b, h: (b, h))
    return pl.pallas_call(
        functools.partial(_sb_kernel, n_blocks=s // SB_BLOCK),
        grid=(n_seq, d_br // HEAD_DIM),
        in_specs=[qo, kv, kv],
        out_specs=qo,
        out_shape=jax.ShapeDtypeStruct((m, d_br), BF16),
        scratch_shapes=[pltpu.VMEM((s, HEAD_DIM), BF16), pltpu.VMEM((HEAD_DIM, s), BF16),
                        pltpu.VMEM((s // SB_BLOCK, HEAD_DIM, SB_BLOCK), F32),
                        pltpu.VMEM((s // SB_BLOCK, 1, SB_BLOCK), F32)],
        compiler_params=_cparams(("parallel", "parallel")),
        name="sb_attention",
    )(q, k, v)


MOBA_BLOCK = 256
MOBA_TOPK = 3
NEG_BIG = -1e30


def _moba_kernel(q_ref, k_ref, v_ref, o_ref, vt_ref, s_ref, *, n_blocks):
    blk = MOBA_BLOCK
    nb_pad = -(-n_blocks // 8) * 8
    k = k_ref[...]
    kb = k.astype(BF16)
    vt_ref[...] = v_ref[...].T.astype(BF16)
    means = [jnp.mean(k[n * blk:(n + 1) * blk], axis=0, keepdims=True) for n in range(n_blocks)]
    if nb_pad > n_blocks:
        means.append(jnp.zeros((nb_pad - n_blocks, k.shape[1]), F32))
    km_hi, km_lo = _split_bf16(jnp.concatenate(means, axis=0))
    q = q_ref[...]
    q_hi, q_lo = _split_bf16(q)
    gate = _dot_nt(km_hi, q_hi) + (_dot_nt(km_hi, q_lo) + _dot_nt(km_lo, q_hi))
    qs = (q * (HEAD_DIM ** -0.5)).astype(BF16)

    blk_id = lax.broadcasted_iota(jnp.int32, (nb_pad, blk), 0)
    key = lax.broadcasted_iota(jnp.int32, (blk, blk), 0)
    qry = lax.broadcasted_iota(jnp.int32, (blk, blk), 1)
    for i in range(n_blocks):
        own = slice(i * blk, (i + 1) * blk)
        n = (i + 1) * blk
        st = _dot_nt(kb[:n], qs[own])
        g = gate[:, own]
        for j in range(i):
            gj = g[j:j + 1, :]
            ahead = jnp.logical_and(blk_id < i, jnp.logical_or(g > gj, jnp.logical_and(g == gj, blk_id < j)))
            rank = jnp.sum(jnp.where(ahead, 1.0, 0.0), axis=0, keepdims=True)
            rows = slice(j * blk, (j + 1) * blk)
            s_ref[rows, :] = jnp.where(rank < MOBA_TOPK, st[rows], NEG_BIG)
        s_ref[own, :] = jnp.where(key <= qry, st[own], NEG_BIG)
        s = s_ref[:n, :]
        m = jnp.max(s, axis=0, keepdims=True)
        p = jnp.exp(s - m)
        l = jnp.sum(p, axis=0, keepdims=True)
        ot = jnp.dot(vt_ref[:, :n], p.astype(BF16), preferred_element_type=F32)
        o_ref[own, :] = (ot / l).T.astype(o_ref.dtype)


def moba_attention(q, k, v, *, n_seq, layer):
    m, d_br = q.shape
    s = m // n_seq
    kv = pl.BlockSpec((None, s, HEAD_DIM), lambda b, h: (layer, b, h))
    qo = pl.BlockSpec((s, HEAD_DIM), lambda b, h: (b, h))
    return pl.pallas_call(
        functools.partial(_moba_kernel, n_blocks=s // MOBA_BLOCK),
        grid=(n_seq, d_br // HEAD_DIM),
        in_specs=[qo, kv, kv],
        out_specs=qo,
        out_shape=jax.ShapeDtypeStruct((m, d_br), BF16),
        scratch_shapes=[pltpu.VMEM((HEAD_DIM, s), BF16), pltpu.VMEM((s, MOBA_BLOCK), F32)],
        compiler_params=_cparams(("parallel", "parallel")),
        name="moba_attention",
    )(q, k, v)


def _sb_step_kernel(pt_ref, q_ref, kc_ref, vc_ref, o_ref, kbuf, vbuf, sem, acc_ref, carry_ref, *, layer, n_pages):
    b = pl.program_id(0)
    page_len, n_heads, d = kbuf.shape[1:]
    flat = page_len * n_heads
    n_chunks = flat // LANES
    scale = HEAD_DIM ** -0.5

    def copies(p, slot):
        page = pt_ref[b * n_pages + p]
        return (pltpu.make_async_copy(kc_ref.at[layer, page], kbuf.at[slot], sem.at[0, slot]),
                pltpu.make_async_copy(vc_ref.at[layer, page], vbuf.at[slot], sem.at[1, slot]))

    def start(p, slot):
        for c in copies(p, slot):
            c.start()

    def wait(p, slot):
        for c in copies(p, slot):
            c.wait()

    q = q_ref[...].astype(BF16)
    sub = lax.broadcasted_iota(jnp.int32, (n_heads, flat), 0)
    lane = lax.broadcasted_iota(jnp.int32, (n_heads, flat), 1)
    own_head = (lane % n_heads) == sub
    r2 = lax.broadcasted_iota(jnp.int32, (LANES, LANES), 0)
    c2 = lax.broadcasted_iota(jnp.int32, (LANES, LANES), 1)
    after = jnp.where(r2 > c2, 1.0, 0.0).astype(BF16)
    acc_ref[...] = jnp.zeros_like(acc_ref)
    carry_ref[...] = jnp.zeros_like(carry_ref)
    start(n_pages - 1, 0)

    def body(state):
        p, _ = state
        slot = (n_pages - 1 - p) % 2

        @pl.when(p > 0)
        def _():
            start(p - 1, 1 - slot)

        wait(p, slot)
        kf = kbuf[slot].reshape(flat, d).astype(BF16)
        vf = vbuf[slot].reshape(flat, d).astype(BF16)
        z = _dot_nt(q, kf) * scale
        log_take = _log_sigmoid(z)
        log_stay = jnp.where(own_head, log_take - z, 0.0)
        stacked = jnp.concatenate([log_stay[:, c * LANES:(c + 1) * LANES] for c in range(n_chunks)], axis=0)
        hi, lo = _split_bf16(stacked)
        within = jnp.dot(hi, after, preferred_element_type=F32) + jnp.dot(lo, after, preferred_element_type=F32)
        total = jnp.sum(stacked, axis=1, keepdims=True)
        run = carry_ref[...]
        pieces = [None] * n_chunks
        for c in range(n_chunks - 1, -1, -1):
            rows = slice(c * n_heads, (c + 1) * n_heads)
            lanes = slice(c * LANES, (c + 1) * LANES)
            later = run + within[rows, :]
            pieces[c] = jnp.where(own_head[:, lanes], jnp.exp(log_take[:, lanes] + later), 0.0)
            run = run + total[rows, :]
        a = jnp.concatenate(pieces, axis=1).astype(BF16)
        acc_ref[...] += jnp.dot(a, vf, preferred_element_type=F32)
        carry_ref[...] = run
        return p - 1, jnp.max(run) > DEAD_LOG

    p_end, _ = lax.while_loop(lambda s: jnp.logical_and(s[0] >= 0, s[1]), body, (n_pages - 1, True))

    @pl.when(p_end >= 0)
    def _():
        wait(p_end, (n_pages - 1 - p_end) % 2)

    o_ref[...] = acc_ref[...]


def sb_step(q, k_cache, v_cache, page_table, *, layer):
    n_seq, n_heads, d = q.shape
    n_pages = page_table.shape[1]
    page_len = k_cache.shape[2]
    tok = pl.BlockSpec((None, n_heads, d), lambda b, pt: (b, 0, 0))
    return pl.pallas_call(
        functools.partial(_sb_step_kernel, layer=layer, n_pages=n_pages),
        grid_spec=pltpu.PrefetchScalarGridSpec(
            num_scalar_prefetch=1,
            grid=(n_seq,),
            in_specs=[tok, pl.BlockSpec(memory_space=pl.ANY), pl.BlockSpec(memory_space=pl.ANY)],
            out_specs=tok,
            scratch_shapes=[pltpu.VMEM((2, page_len, n_heads, d), F32), pltpu.VMEM((2, page_len, n_heads, d), F32),
                            pltpu.SemaphoreType.DMA((2, 2)),
                            pltpu.VMEM((n_heads, d), F32), pltpu.VMEM((n_heads, 1), F32)]),
        out_shape=jax.ShapeDtypeStruct((n_seq, n_heads, d), F32),
        compiler_params=_cparams(("arbitrary",)),
        name="sb_step",
    )(page_table.reshape(-1), q, k_cache, v_cache)


GATE_PAGES = 8


def _moba_gate_kernel(pt_ref, q_ref, *refs, pages_per_block, n_blocks):
    pages = refs[:GATE_PAGES]
    idx_ref, g_ref = refs[GATE_PAGES:]
    s = pl.program_id(1)
    per_step = GATE_PAGES // pages_per_block
    block_len = pages_per_block * pages[0].shape[0]

    @pl.when(s == 0)
    def _():
        g_ref[...] = jnp.full_like(g_ref, -jnp.inf)

    q = q_ref[...]
    lane = lax.broadcasted_iota(jnp.int32, g_ref.shape, 1)
    g = g_ref[...]
    for n in range(per_step):
        ksum = jnp.sum(pages[n * pages_per_block][...], axis=0)
        for r in range(1, pages_per_block):
            ksum = ksum + jnp.sum(pages[n * pages_per_block + r][...], axis=0)
        gn = jnp.sum(ksum * q, axis=1, keepdims=True) * (1.0 / block_len)
        g = jnp.where(lane == s * per_step + n, gn, g)
    g_ref[...] = g

    @pl.when(s == pl.num_programs(1) - 1)
    def _():
        lane_f = lane.astype(F32)
        gg = g
        out = jnp.zeros(g_ref.shape, jnp.int32)
        for r in range(MOBA_TOPK):
            best = jnp.max(gg, axis=1, keepdims=True)
            first = jnp.min(jnp.where(gg == best, lane_f, float(LANES)), axis=1, keepdims=True)
            out = jnp.where(lane == r, first.astype(jnp.int32), out)
            gg = jnp.where(lane_f == first, -jnp.inf, gg)
        idx_ref[...] = out


def moba_gate(q, k_cache, page_table, *, layer):
    n_seq, n_heads, d = q.shape
    n_pages = page_table.shape[1]
    page_len = k_cache.shape[2]
    pages_per_block = MOBA_BLOCK // page_len
    n_blocks = n_pages // pages_per_block
    assert MOBA_TOPK <= n_blocks <= LANES and n_pages % GATE_PAGES == 0 and GATE_PAGES % pages_per_block == 0
    tok = pl.BlockSpec((None, n_heads, d), lambda b, s, pt: (b, 0, 0))
    page_specs = [
        pl.BlockSpec((None, None, page_len, n_heads, d),
                     functools.partial(lambda b, s, pt, r: (layer, pt[b * n_pages + s * GATE_PAGES + r], 0, 0, 0), r=r))
        for r in range(GATE_PAGES)]
    return pl.pallas_call(
        functools.partial(_moba_gate_kernel, pages_per_block=pages_per_block, n_blocks=n_blocks),
        grid_spec=pltpu.PrefetchScalarGridSpec(
            num_scalar_prefetch=1,
            grid=(n_seq, n_pages // GATE_PAGES),
            in_specs=[tok] + page_specs,
            out_specs=pl.BlockSpec((None, n_heads, LANES), lambda b, s, pt: (b, 0, 0)),
            scratch_shapes=[pltpu.VMEM((n_heads, LANES), F32)]),
        out_shape=jax.ShapeDtypeStruct((n_seq, n_heads, LANES), jnp.int32),
        compiler_params=_cparams(("parallel", "arbitrary")),
        name="moba_gate",
    )(page_table.reshape(-1), q, *([k_cache] * GATE_PAGES))


def _moba_step_kernel(pt_ref, sel_ref, q_ref, kn_ref, vn_ref, kc_ref, vc_ref, o_ref, kg, vg, sem, *,
                      layer, n_pages, pages_per_block):
    b = pl.program_id(0)
    n_heads, d = q_ref.shape
    page_len = kg.shape[1] // (MOBA_TOPK * pages_per_block)
    scale = HEAD_DIM ** -0.5

    def copies():
        out = []
        for h in range(n_heads):
            for r in range(MOBA_TOPK):
                blk = sel_ref[(b * n_heads + h) * MOBA_TOPK + r]
                for g in range(pages_per_block):
                    page = pt_ref[b * n_pages + blk * pages_per_block + g]
                    rows = pl.ds((r * pages_per_block + g) * page_len, page_len)
                    out.append(pltpu.make_async_copy(kc_ref.at[layer, page, :, h, :], kg.at[h, rows, :], sem.at[0]))
                    out.append(pltpu.make_async_copy(vc_ref.at[layer, page, :, h, :], vg.at[h, rows, :], sem.at[1]))
        return out

    cps = copies()
    for c in cps:
        c.start()
    for c in cps:
        c.wait()

    qf = q_ref[...]
    q = qf.astype(BF16)
    sub = lax.broadcasted_iota(jnp.int32, (n_heads, d), 0)
    s_own = jnp.sum(qf * kn_ref[...], axis=1, keepdims=True) * scale
    res = jnp.zeros((n_heads, d), F32)
    for h in range(n_heads):
        s = _dot_nt(q, kg[h].astype(BF16)) * scale
        m = jnp.maximum(jnp.max(s, axis=1, keepdims=True), s_own)
        p = jnp.exp(s - m)
        p_own = jnp.exp(s_own - m)
        l = jnp.sum(p, axis=1, keepdims=True) + p_own
        o = (jnp.dot(p.astype(BF16), vg[h].astype(BF16), preferred_element_type=F32) + p_own * vn_ref[...]) / l
        res = jnp.where(sub == h, o, res)
    o_ref[...] = res


def moba_step(q, k_new, v_new, sel, k_cache, v_cache, page_table, *, layer):
    n_seq, n_heads, d = q.shape
    n_pages = page_table.shape[1]
    page_len = k_cache.shape[2]
    pages_per_block = MOBA_BLOCK // page_len
    tok = pl.BlockSpec((None, n_heads, d), lambda b, pt, sl: (b, 0, 0))
    gathered = pltpu.VMEM((n_heads, MOBA_TOPK * MOBA_BLOCK, d), F32)
    return pl.pallas_call(
        functools.partial(_moba_step_kernel, layer=layer, n_pages=n_pages, pages_per_block=pages_per_block),
        grid_spec=pltpu.PrefetchScalarGridSpec(
            num_scalar_prefetch=2,
            grid=(n_seq,),
            in_specs=[tok, tok, tok, pl.BlockSpec(memory_space=pl.ANY), pl.BlockSpec(memory_space=pl.ANY)],
            out_specs=tok,
            scratch_shapes=[gathered, gathered, pltpu.SemaphoreType.DMA((2,))]),
        out_shape=jax.ShapeDtypeStruct((n_seq, n_heads, d), F32),
        compiler_params=_cparams(("arbitrary",)),
        name="moba_step",
    )(page_table.reshape(-1), sel[:, :, :MOBA_TOPK].reshape(-1), q, k_new, v_new, k_cache, v_cache)


def _conv_step_kernel(st_ref, glu_ref, cw_ref, cb_ref, lg_ref, lb_ref, y_ref, ns_ref, *, width):
    n_seq = st_ref.shape[0]
    past = width - 1
    rows = []
    for b in range(n_seq):
        st = st_ref[b]
        new = glu_ref[b:b + 1, :]
        co = jnp.sum(st * cw_ref[0:past, :], axis=0, keepdims=True) + new * cw_ref[past:width, :] + cb_ref[...]
        rows.append(_ln_swish(co, lg_ref[...], lb_ref[...]))
        ns_ref[b, 0:past - 1, :] = st[1:past]
        ns_ref[b, past - 1:past, :] = new
    y_ref[...] = jnp.concatenate(rows, axis=0).astype(y_ref.dtype)


def conv_step(state, glu, conv_w, conv_b, ln_g, ln_b):
    n_seq, past, ch = state.shape
    width = conv_w.shape[0]
    return pl.pallas_call(
        functools.partial(_conv_step_kernel, width=width),
        out_shape=[jax.ShapeDtypeStruct((n_seq, ch), BF16), jax.ShapeDtypeStruct((n_seq, past, ch), F32)],
        compiler_params=pltpu.CompilerParams(vmem_limit_bytes=VMEM_LIMIT),
        name="conv_step",
    )(state, glu, conv_w, conv_b.reshape(1, ch), ln_g.reshape(1, ch), ln_b.reshape(1, ch))


PROMPT_ROWS = 1024


def kernel(x_prompt, x_sample, p_prompt, p_sample, cache_sb_k, cache_sb_v, cache_mb_k, cache_mb_v, state_conv,
           page_table, norm_ffn1, w_ff1_in, w_ff1_out, norm_mix, w_in, conv_w, conv_b, conv_ln_g, conv_ln_b, w_br,
           w_gate, b_gate, w_o, norm_ffn2, w_ff2_in, w_ff2_out, norm_ple, w_pg, w_pe, norm_final):
    depth = w_in.shape[0]
    n_seq, seq, d = x_prompt.shape
    n_dec, dec_seq, _ = x_sample.shape
    past = state_conv.shape[2]
    d_br = w_br.shape[2]
    n_heads = d_br // HEAD_DIM
    assert dec_seq == 1 and seq >= past and seq % MOBA_BLOCK == 0
    m = n_seq * seq
    tm = min(PROMPT_ROWS, m)

    def trunk_layer(layer, h, p, rows, kv_bufs, branches):
        h = ffn(h, norm_ffn1[layer], w_ff1_in, w_ff1_out, layer=layer, tm=rows)
        glu, q_sb, k_sb, v_sb, q_mb, k_mb, v_mb = inproj(h, norm_mix[layer], w_in, layer=layer, tm=rows, kv_bufs=kv_bufs)
        kv_bufs = (k_sb, v_sb, k_mb, v_mb)
        y_a, y_b, y_c, aux = branches(glu, q_sb, q_mb, kv_bufs)
        mixed = mix(h, norm_mix[layer], y_a, y_b, y_c, w_gate, b_gate[layer], w_br, layer=layer, tm=rows)
        h = matres(mixed, w_o, h, layer=layer, tm=rows)
        h = ffn(h, norm_ffn2[layer], w_ff2_in, w_ff2_out, layer=layer, tm=rows)
        h = ple(h, norm_ple[layer], p, w_pg, w_pe, layer=layer, tm=rows)
        return h, kv_bufs, aux

    hp = x_prompt.reshape(m, d)
    hs = x_sample.reshape(n_dec, d)
    pp = p_prompt.reshape(depth, m, -1)
    ps = p_sample.reshape(depth, n_dec, -1)
    kv_p = (jnp.zeros((depth, m, d_br), F32),) * 4
    kv_s = (jnp.zeros((depth, n_dec, d_br), F32),) * 4
    conv_p, conv_s = [], []
    for layer in range(depth):
        conv_args = (conv_w[layer], conv_b[layer], conv_ln_g[layer], conv_ln_b[layer])

        def prompt_branches(glu, q_sb, q_mb, kv, layer=layer, conv_args=conv_args):
            y_a = conv_branch(glu, *conv_args, n_seq=n_seq)
            y_b = sb_attention(q_sb, kv[0], kv[1], n_seq=n_seq, layer=layer)
            y_c = moba_attention(q_mb, kv[2], kv[3], n_seq=n_seq, layer=layer)
            return y_a, y_b, y_c, glu.reshape(n_seq, seq, d_br)[:, seq - past:]

        def sample_branches(glu, q_sb, q_mb, kv, layer=layer, conv_args=conv_args):
            y_a, new_state = conv_step(state_conv[layer], glu, *conv_args)
            tok = (n_dec, n_heads, HEAD_DIM)
            y_b = sb_step(q_sb.astype(F32).reshape(tok), cache_sb_k, cache_sb_v, page_table, layer=layer)
            q3 = q_mb.reshape(tok)
            sel = moba_gate(q3, cache_mb_k, page_table, layer=layer)
            y_c = moba_step(q3, kv[2][layer].reshape(tok), kv[3][layer].reshape(tok), sel, cache_mb_k, cache_mb_v,
                            page_table, layer=layer)
            return y_a, y_b.reshape(n_dec, d_br).astype(BF16), y_c.reshape(n_dec, d_br).astype(BF16), new_state

        hp, kv_p, aux_p = trunk_layer(layer, hp, pp, tm, kv_p, prompt_branches)
        hs, kv_s, aux_s = trunk_layer(layer, hs, ps, n_dec, kv_s, sample_branches)
        conv_p.append(aux_p)
        conv_s.append(aux_s)

    y_prompt = rmsnorm(hp, norm_final, tm=tm).reshape(n_seq, seq, d)
    y_sample = rmsnorm(hs, norm_final, tm=n_dec).reshape(n_dec, 1, d)
    rows_p = [a.reshape(depth, n_seq, seq, n_heads, HEAD_DIM) for a in kv_p]
    rows_s = [a.reshape(depth, n_dec, 1, n_heads, HEAD_DIM) for a in kv_s]
    return (y_prompt, y_sample, *rows_p, jnp.stack(conv_p), *rows_s, jnp.stack(conv_s))
```
